```python
import jax, jax.numpy as jnp
from jax import lax
import numpy as np

D_MODEL = 4096
BATCH = 4
SEQ = 4096
DEPTH = 1
DEC_BATCH = 8
DEC_SEQ = 64
PAST_LEN = 4096

CHUNK = 64
N_HEADS = 32
Q_LORA = 1024
KV_LORA = 512
NOPE_DIM = 128
ROPE_DIM = 64
QK_DIM = NOPE_DIM + ROPE_DIM
V_DIM = 128
ROPE_THETA = 10000.0
ATTN_SCALE = QK_DIM ** -0.5
QBLK = 128
RNN_WIDTH = D_MODEL
RNN_BLOCKS = 16
RNN_BLOCK_W = RNN_WIDTH // RNN_BLOCKS
CONV_W = 4
LRU_C = 8.0
N_EXPERTS = 32
TOP_K = 4
D_FF = D_MODEL
SWIGLU_LIMIT = 7.0
SWIGLU_ALPHA = 1.702
EBLK = 128
NORM_EPS = 1e-6
NEG_INF = -1e30
SPLIT_AT = (Q_LORA, Q_LORA + KV_LORA, Q_LORA + KV_LORA + ROPE_DIM,
            Q_LORA + KV_LORA + ROPE_DIM + RNN_WIDTH, Q_LORA + KV_LORA + ROPE_DIM + 2 * RNN_WIDTH)
IN_DIM = Q_LORA + KV_LORA + ROPE_DIM + 2 * RNN_WIDTH + 2 * D_MODEL

kernel_name = 'mla_rglru_gated_moe_stream_step'


def _rmsnorm(x, g):
    xf = x.astype(jnp.float32)
    y = xf * lax.rsqrt(jnp.mean(xf * xf, axis=-1, keepdims=True) + NORM_EPS)
    return (y * g.astype(jnp.float32)).astype(x.dtype)


def _rope(x, pos):
    half = ROPE_DIM // 2
    inv = ROPE_THETA ** (-jnp.arange(half, dtype=jnp.float32) / half)
    ang = pos.astype(jnp.float32)[:, None] * inv[None, :]
    ang = ang.reshape((pos.shape[0],) + (1,) * (x.ndim - 3) + (half,))
    cos, sin = jnp.cos(ang), jnp.sin(ang)
    xf = x.astype(jnp.float32)
    x1, x2 = xf[..., :half], xf[..., half:]
    return jnp.concatenate([x1 * cos - x2 * sin, x2 * cos + x1 * sin], axis=-1).astype(x.dtype)


def _attend(q_nope, q_rope, k_nope, k_rope, v, q_pos, k_pos):
    s = (jnp.einsum('bqhd,bkhd->bhqk', q_nope, k_nope)
         + jnp.einsum('bqhr,bkr->bhqk', q_rope, k_rope)).astype(jnp.float32) * ATTN_SCALE
    allowed = (k_pos[None, :] // CHUNK) <= (q_pos[:, None] // CHUNK)
    s = jnp.where(allowed[None, None], s, NEG_INF)
    p = jax.nn.softmax(s, axis=-1).astype(v.dtype)
    return jnp.einsum('bhqk,bkhd->bqhd', p, v)


def _mla(c_q, c_kv, k_r, pos, past_ckv, past_krope, g_q, w_uq, g_kv, w_uk, w_uv, w_o_attn):
    b, l, _ = c_q.shape
    q = jnp.einsum('blr,rhd->blhd', _rmsnorm(c_q, g_q), w_uq)
    q_nope, q_rope = q[..., :NOPE_DIM], _rope(q[..., NOPE_DIM:], pos)
    ckv_new = _rmsnorm(c_kv, g_kv)
    kr_new = _rope(k_r, pos)
    if past_ckv is None:
        ckv_all, kr_all, k_pos = ckv_new, kr_new, pos
    else:
        ckv_all = jnp.concatenate([past_ckv.astype(ckv_new.dtype), ckv_new], axis=1)
        kr_all = jnp.concatenate([past_krope.astype(kr_new.dtype), kr_new], axis=1)
        k_pos = jnp.arange(ckv_all.shape[1], dtype=jnp.int32)
    k_nope = jnp.einsum('bkc,chd->bkhd', ckv_all, w_uk)
    v = jnp.einsum('bkc,chd->bkhd', ckv_all, w_uv)
    if l <= QBLK:
        o = _attend(q_nope, q_rope, k_nope, kr_all, v, pos, k_pos)
    else:
        nb = l // QBLK
        def to_blocks(a):
            return jnp.moveaxis(a.reshape((b, nb, QBLK) + a.shape[2:]), 1, 0)
        o = lax.map(lambda xs: _attend(xs[0], xs[1], k_nope, kr_all, v, xs[2], k_pos),
                    (to_blocks(q_nope), to_blocks(q_rope), pos.reshape(nb, QBLK)))
        o = jnp.moveaxis(o, 0, 1).reshape(b, l, N_HEADS, V_DIM)
    out = o.reshape(b, l, N_HEADS * V_DIM) @ w_o_attn
    return out, ckv_new, kr_new


def _scan_op(left, right):
    a1, b1 = left
    a2, b2 = right
    return a1 * a2, a2 * b1 + b2


def _rglru_branch(xb, yb, pos, conv_past, h0, conv_w, conv_b, w_ra, b_ra, w_ri, b_ri, lam, w_o_rnn):
    b, l, c = xb.shape
    xpad = jnp.concatenate([conv_past.astype(xb.dtype), xb], axis=1)
    xc = conv_b
    for j in range(CONV_W):
        xc = xc + xpad[:, j:j + l] * conv_w[j]
    xcb = xc.reshape(b, l, RNN_BLOCKS, RNN_BLOCK_W)
    r = jax.nn.sigmoid((jnp.einsum('blnc,ncd->blnd', xcb, w_ra).reshape(b, l, c) + b_ra).astype(jnp.float32))
    i = jax.nn.sigmoid((jnp.einsum('blnc,ncd->blnd', xcb, w_ri).reshape(b, l, c) + b_ri).astype(jnp.float32))
    log_a = -LRU_C * r * jax.nn.softplus(-lam.astype(jnp.float32))
    a = jnp.exp(log_a)
    mult = jnp.sqrt(-jnp.expm1(2.0 * log_a))
    reset = (pos == 0)[None, :, None]
    a = jnp.where(reset, 0.0, a)
    mult = jnp.where(reset, 1.0, mult)
    bterm = mult * i * xc.astype(jnp.float32)
    a_cum, b_cum = lax.associative_scan(_scan_op, (a, bterm), axis=1)
    h = a_cum * h0.astype(jnp.float32)[:, None, :] + b_cum
    out = (h.astype(xb.dtype) * jax.nn.gelu(yb)) @ w_o_rnn
    return out, xpad[:, -(CONV_W - 1):], h[:, -1].astype(xb.dtype)


def _moe(x2d, w_router, b_router, w_gate, b_gate, w_up, b_up, w_down, b_down):
    t, d = x2d.shape
    logits = x2d.astype(jnp.float32) @ w_router.astype(jnp.float32) + b_router.astype(jnp.float32)
    top_v, top_i = lax.top_k(logits, TOP_K)
    probs = jax.nn.softmax(top_v, axis=-1)
    n_assign = t * TOP_K
    flat_e = top_i.reshape(-1).astype(jnp.int32)
    flat_t = jnp.repeat(jnp.arange(t, dtype=jnp.int32), TOP_K)
    flat_p = probs.reshape(-1)
    order = jnp.argsort(flat_e)
    se, st, sp = flat_e[order], flat_t[order], flat_p[order]
    counts = jnp.bincount(flat_e, length=N_EXPERTS)
    padded = (counts + EBLK - 1) // EBLK * EBLK
    start = jnp.cumsum(counts) - counts
    pstart = jnp.cumsum(padded) - padded
    dest = pstart[se] + jnp.arange(n_assign, dtype=jnp.int32) - start[se]
    n_rows = (n_assign + EBLK - 1) // EBLK * EBLK + N_EXPERTS * EBLK
    xbuf = jnp.zeros((n_rows, d), x2d.dtype).at[dest].set(x2d[st])
    ebuf = jnp.full((n_rows,), N_EXPERTS - 1, jnp.int32).at[dest].set(se)
    blk_e = ebuf[::EBLK]
    xblk = xbuf.reshape(n_rows // EBLK, EBLK, d)

    def expert_rows(args):
        xr, e = args
        g = xr @ w_gate[e] + b_gate[e]
        u = xr @ w_up[e] + b_up[e]
        g = jnp.minimum(g, SWIGLU_LIMIT)
        u = jnp.clip(u, -SWIGLU_LIMIT, SWIGLU_LIMIT)
        hdn = (u + 1.0) * (g * jax.nn.sigmoid(SWIGLU_ALPHA * g))
        return hdn @ w_down[e] + b_down[e]

    ybuf = lax.map(expert_rows, (xblk, blk_e)).reshape(n_rows, d)
    out = jnp.zeros((t, d), jnp.float32).at[st].add(ybuf[dest].astype(jnp.float32) * sp[:, None])
    return out.astype(x2d.dtype)


def _layer(x, pos, past_ckv, past_krope, conv_past, h0, lw):
    (g_mix, w_in, g_q, w_uq, g_kv, w_uk, w_uv, w_o_attn, conv_w, conv_b, w_ra, b_ra, w_ri, b_ri,
     lam, w_o_rnn, w_out, g_ffn, w_router, b_router, w_gate, b_gate, w_up, b_up, w_down, b_down) = lw
    b, l, d = x.shape
    proj = _rmsnorm(x, g_mix) @ w_in
    c_q, c_kv, k_r, xb, yb, gate_logits = jnp.split(proj, SPLIT_AT, axis=-1)
    o_a, ckv_new, kr_new = _mla(c_q, c_kv, k_r, pos, past_ckv, past_krope, g_q, w_uq, g_kv, w_uk, w_uv, w_o_attn)
    o_r, conv_new, h_new = _rglru_branch(xb, yb, pos, conv_past, h0, conv_w, conv_b, w_ra, b_ra, w_ri, b_ri, lam, w_o_rnn)
    gates = jax.nn.sigmoid(gate_logits.astype(jnp.float32)).reshape(b, l, 2, d)
    mix = (gates[:, :, 0] * o_a.astype(jnp.float32) + gates[:, :, 1] * o_r.astype(jnp.float32)).astype(x.dtype)
    h = x + mix @ w_out
    hn = _rmsnorm(h, g_ffn).reshape(b * l, d)
    h = h + _moe(hn, w_router, b_router, w_gate, b_gate, w_up, b_up, w_down, b_down).reshape(b, l, d)
    return h, (ckv_new, kr_new, conv_new, h_new)


def setup_inputs(seed: int = 0) -> dict:
    key = jax.random.key(seed)
    k = jax.random.split(key, 34)
    f32 = jnp.float32

    def nrm(i, shape, scale=1.0):
        return jax.random.normal(k[i], shape, f32) * scale

    def gain(i, shape):
        return 1.0 + 0.02 * jax.random.normal(k[i], shape, f32)

    a0 = jax.random.uniform(k[33], (DEPTH, RNN_WIDTH), f32, 0.9, 0.999)
    s = a0 ** (1.0 / LRU_C)
    return {
        'x_prompt': nrm(0, (BATCH, SEQ, D_MODEL)),
        'x_sample': nrm(1, (DEC_BATCH, DEC_SEQ, D_MODEL)),
        'cache_ckv': nrm(2, (DEPTH, DEC_BATCH, PAST_LEN, KV_LORA)),
        'cache_krope': nrm(3, (DEPTH, DEC_BATCH, PAST_LEN, ROPE_DIM)),
        'state_conv': nrm(4, (DEPTH, DEC_BATCH, CONV_W - 1, RNN_WIDTH)),
        'state_lru': nrm(5, (DEPTH, DEC_BATCH, RNN_WIDTH), 0.5),
        'g_mix': gain(6, (DEPTH, D_MODEL)),
        'w_in': nrm(7, (DEPTH, D_MODEL, IN_DIM), D_MODEL ** -0.5),
        'g_q': gain(8, (DEPTH, Q_LORA)),
        'w_uq': nrm(9, (DEPTH, Q_LORA, N_HEADS, QK_DIM), Q_LORA ** -0.5),
        'g_kv': gain(10, (DEPTH, KV_LORA)),
        'w_uk': nrm(11, (DEPTH, KV_LORA, N_HEADS, NOPE_DIM), KV_LORA ** -0.5),
        'w_uv': nrm(12, (DEPTH, KV_LORA, N_HEADS, V_DIM), KV_LORA ** -0.5),
        'w_o_attn': nrm(13, (DEPTH, N_HEADS * V_DIM, D_MODEL), (N_HEADS * V_DIM) ** -0.5),
        'conv_w': nrm(14, (DEPTH, CONV_W, RNN_WIDTH), CONV_W ** -0.5),
        'conv_b': nrm(15, (DEPTH, RNN_WIDTH), 0.02),
        'w_ra': nrm(16, (DEPTH, RNN_BLOCKS, RNN_BLOCK_W, RNN_BLOCK_W), RNN_BLOCK_W ** -0.5),
        'b_ra': nrm(17, (DEPTH, RNN_WIDTH), 0.02),
        'w_ri': nrm(18, (DEPTH, RNN_BLOCKS, RNN_BLOCK_W, RNN_BLOCK_W), RNN_BLOCK_W ** -0.5),
        'b_ri': nrm(19, (DEPTH, RNN_WIDTH), 0.02),
        'lam': jnp.log(s) - jnp.log1p(-s),
        'w_o_rnn': nrm(20, (DEPTH, RNN_WIDTH, D_MODEL), RNN_WIDTH ** -0.5),
        'w_out': nrm(21, (DEPTH, D_MODEL, D_MODEL), D_MODEL ** -0.5),
        'g_ffn': gain(22, (DEPTH, D_MODEL)),
        'w_router': nrm(23, (DEPTH, D_MODEL, N_EXPERTS), D_MODEL ** -0.5),
        'b_router': nrm(24, (DEPTH, N_EXPERTS), 0.01),
        'w_gate': nrm(25, (DEPTH, N_EXPERTS, D_MODEL, D_FF), D_MODEL ** -0.5),
        'b_gate': nrm(26, (DEPTH, N_EXPERTS, D_FF), 0.02),
        'w_up': nrm(27, (DEPTH, N_EXPERTS, D_MODEL, D_FF), D_MODEL ** -0.5),
        'b_up': nrm(28, (DEPTH, N_EXPERTS, D_FF), 0.02),
        'w_down': nrm(29, (DEPTH, N_EXPERTS, D_FF, D_MODEL), D_FF ** -0.5),
        'b_down': nrm(30, (DEPTH, N_EXPERTS, D_MODEL), 0.02),
        'g_final': gain(31, (D_MODEL,)),
    }


def reference(x_prompt, x_sample, cache_ckv, cache_krope, state_conv, state_lru,
              g_mix, w_in, g_q, w_uq, g_kv, w_uk, w_uv, w_o_attn, conv_w, conv_b,
              w_ra, b_ra, w_ri, b_ri, lam, w_o_rnn, w_out, g_ffn, w_router, b_router,
              w_gate, b_gate, w_up, b_up, w_down, b_down, g_final):
    bp, lp, _ = x_prompt.shape
    bs, ls, _ = x_sample.shape
    past = cache_ckv.shape[2]
    pos_p = jnp.arange(lp, dtype=jnp.int32)
    pos_s = past + jnp.arange(ls, dtype=jnp.int32)
    zero_conv = jnp.zeros((bp, CONV_W - 1, RNN_WIDTH), x_prompt.dtype)
    zero_h = jnp.zeros((bp, RNN_WIDTH), x_prompt.dtype)
    hp, hs = x_prompt, x_sample
    st_p, st_s = [], []
    for layer in range(DEPTH):
        lw = tuple(w[layer] for w in (g_mix, w_in, g_q, w_uq, g_kv, w_uk, w_uv, w_o_attn, conv_w, conv_b,
                                      w_ra, b_ra, w_ri, b_ri, lam, w_o_rnn, w_out, g_ffn, w_router, b_router,
                                      w_gate, b_gate, w_up, b_up, w_down, b_down))
        hp, sp_ = _layer(hp, pos_p, None, None, zero_conv, zero_h, lw)
        hs, ss_ = _layer(hs, pos_s, cache_ckv[layer], cache_krope[layer], state_conv[layer], state_lru[layer], lw)
        st_p.append(sp_)
        st_s.append(ss_)
    y_prompt = _rmsnorm(hp, g_final)
    y_sample = _rmsnorm(hs, g_final)
    ckv_prompt = jnp.stack([s[0] for s in st_p])
    krope_prompt = jnp.stack([s[1] for s in st_p])
    conv_prompt = jnp.stack([s[2] for s in st_p])
    lru_prompt = jnp.stack([s[3] for s in st_p])
    ckv_sample = jnp.stack([s[0] for s in st_s])
    krope_sample = jnp.stack([s[1] for s in st_s])
    conv_sample = jnp.stack([s[2] for s in st_s])
    lru_sample = jnp.stack([s[3] for s in st_s])
    return (y_prompt, y_sample, ckv_prompt, krope_prompt, conv_prompt, lru_prompt,
            ckv_sample, krope_sample, conv_sample, lru_sample)
```

```python
import functools
import math

import jax
import jax.numpy as jnp
import numpy as np
from jax import lax
from jax.experimental import pallas as pl
from jax.experimental.pallas import tpu as pltpu

CHUNK = 64
ROPE_THETA = 10000.0
LRU_C = 8.0
TOP_K = 4
SWIGLU_LIMIT = 7.0
SWIGLU_ALPHA = 1.702
NORM_EPS = 1e-6
NEG_INF = -1e30

LANES = 128
SUBLANES = 8
V7X_VMEM_LIMIT_BYTES = 56 * 1024 * 1024

BF16 = jnp.bfloat16
F32 = jnp.float32


def _divisor_tile(n, pref, mult):
    if n <= pref:
        return n
    t = (pref // mult) * mult
    while t >= mult:
        if n % t == 0:
            return t
        t -= mult
    return n


def _params(sem):
    return pltpu.CompilerParams(dimension_semantics=sem, vmem_limit_bytes=V7X_VMEM_LIMIT_BYTES)


def _rms(x, g):
    return x * lax.rsqrt(jnp.mean(x * x, axis=-1, keepdims=True) + NORM_EPS) * g


def _rmsnorm_kernel(x_ref, g_ref, o_ref):
    o_ref[...] = _rms(x_ref[...], g_ref[...]).astype(o_ref.dtype)


def _rmsnorm_bf16(x, g):
    t, d = x.shape
    tm = _divisor_tile(t, 256, SUBLANES)
    return pl.pallas_call(
        _rmsnorm_kernel,
        out_shape=jax.ShapeDtypeStruct((t, d), BF16),
        grid=(t // tm,),
        in_specs=[pl.BlockSpec((tm, d), lambda i: (i, 0)), pl.BlockSpec((1, d), lambda i: (0, 0))],
        out_specs=pl.BlockSpec((tm, d), lambda i: (i, 0)),
        compiler_params=_params(("parallel",)),
        name="rmsnorm",
    )(x, g.reshape(1, d))


def _matmul_kernel(*refs, n_pairs, n_tile, n_row, epilogue):
    a_refs = refs[0:2 * n_pairs:2]
    b_refs = refs[1:2 * n_pairs:2]
    tile_refs = refs[2 * n_pairs:2 * n_pairs + n_tile]
    row_refs = refs[2 * n_pairs + n_tile:2 * n_pairs + n_tile + n_row]
    o_ref = refs[-1]
    accs = [jnp.dot(a[...], b[...], preferred_element_type=F32) for a, b in zip(a_refs, b_refs)]
    o_ref[...] = epilogue(accs, [r[...] for r in tile_refs], [r[...] for r in row_refs]).astype(o_ref.dtype)


def _matmul(pairs, epilogue, out_dtype, tile_extras=(), row_extras=(), tm=512, tn=1024, name="matmul"):
    m = pairs[0][0].shape[0]
    n = pairs[0][1].shape[1]
    tm = _divisor_tile(m, tm, 16)
    tn = _divisor_tile(n, tn, 2 * LANES)
    in_specs, args = [], []
    for a, b in pairs:
        in_specs += [pl.BlockSpec((tm, a.shape[1]), lambda j, i: (i, 0)),
                     pl.BlockSpec((b.shape[0], tn), lambda j, i: (0, j))]
        args += [a, b]
    for e, col0 in tile_extras:
        assert col0 % tn == 0
        in_specs.append(pl.BlockSpec((tm, tn), functools.partial(lambda j, i, jb: (i, j + jb), jb=col0 // tn)))
        args.append(e)
    for e in row_extras:
        in_specs.append(pl.BlockSpec((tm, e.shape[1]), lambda j, i: (i, 0)))
        args.append(e)
    kern = functools.partial(_matmul_kernel, n_pairs=len(pairs), n_tile=len(tile_extras),
                             n_row=len(row_extras), epilogue=epilogue)
    return pl.pallas_call(
        kern,
        out_shape=jax.ShapeDtypeStruct((m, n), out_dtype),
        grid=(n // tn, m // tm),
        in_specs=in_specs,
        out_specs=pl.BlockSpec((tm, tn), lambda j, i: (i, j)),
        compiler_params=_params(("parallel", "parallel")),
        name=name,
    )(*args)


def _ep_identity(accs, tiles, rows):
    return accs[0]


def _ep_gelu(accs, tiles, rows):
    return jax.nn.gelu(accs[0])


def _ep_sigmoid(accs, tiles, rows):
    return jax.nn.sigmoid(accs[0])


def _rope_block(blk, cc, ss):
    return blk * cc + pltpu.roll(blk, LANES // 2, axis=1) * ss


def _ep_q_rope(accs, tiles, rows, *, scale):
    acc = accs[0]
    cc, ss = rows
    out = []
    for h in range(acc.shape[1] // (2 * LANES)):
        base = h * 2 * LANES
        out.append(acc[:, base:base + LANES] * scale)
        out.append(_rope_block(acc[:, base + LANES:base + 2 * LANES], cc, ss) * scale)
    return jnp.concatenate(out, axis=1)


def _ep_merge(accs, tiles, rows):
    return tiles[0].astype(F32) * accs[0] + tiles[1].astype(F32) * accs[1]


def _ep_residual(accs, tiles, rows):
    return tiles[0] + accs[0]


def _head_kernel(x_ref, w_ref, gq_ref, gkv_ref, cc_ref, ss_ref,
                 cqn_ref, ckv_ref, ckvb_ref, kr_ref, krb_ref, *, q_lora, kv_lora):
    p = jnp.dot(x_ref[...], w_ref[...], preferred_element_type=F32)
    cqn_ref[...] = _rms(p[:, :q_lora], gq_ref[...]).astype(cqn_ref.dtype)
    ckv = _rms(p[:, q_lora:q_lora + kv_lora], gkv_ref[...])
    ckv_ref[...] = ckv
    ckvb_ref[...] = ckv.astype(ckvb_ref.dtype)
    kr = _rope_block(p[:, q_lora + kv_lora:], cc_ref[...], ss_ref[...])
    kr_ref[...] = kr
    krb_ref[...] = kr.astype(krb_ref.dtype)


def _head_proj(xn, w_head, g_q, g_kv, cc, ss):
    t, d = xn.shape
    q_lora, kv_lora = g_q.shape[0], g_kv.shape[0]
    nh = w_head.shape[1]
    tm = _divisor_tile(t, 512, 16)
    row = lambda i: (i, 0)
    fixed = lambda i: (0, 0)
    return pl.pallas_call(
        functools.partial(_head_kernel, q_lora=q_lora, kv_lora=kv_lora),
        out_shape=(jax.ShapeDtypeStruct((t, q_lora), BF16),
                   jax.ShapeDtypeStruct((t, kv_lora), F32),
                   jax.ShapeDtypeStruct((t, kv_lora), BF16),
                   jax.ShapeDtypeStruct((t, LANES), F32),
                   jax.ShapeDtypeStruct((t, LANES), BF16)),
        grid=(t // tm,),
        in_specs=[pl.BlockSpec((tm, d), row), pl.BlockSpec((d, nh), fixed),
                  pl.BlockSpec((1, q_lora), fixed), pl.BlockSpec((1, kv_lora), fixed),
                  pl.BlockSpec((tm, LANES), row), pl.BlockSpec((tm, LANES), row)],
        out_specs=(pl.BlockSpec((tm, q_lora), row), pl.BlockSpec((tm, kv_lora), row),
                   pl.BlockSpec((tm, kv_lora), row), pl.BlockSpec((tm, LANES), row),
                   pl.BlockSpec((tm, LANES), row)),
        compiler_params=_params(("parallel",)),
        name="head_proj",
    )(xn, w_head, g_q.reshape(1, -1), g_kv.reshape(1, -1), cc, ss)


def _attn_kernel(q_ref, k_ref, kr_ref, v_ref, o_ref, m_scr, l_scr, acc_scr, *, lq, lk, tq, tk, q_pos0):
    nq = lq // tq
    nk_total = lk // tk

    def q_block(qi, carry):
        r0 = pl.multiple_of(qi * tq, tq)
        q = q_ref[pl.ds(r0, tq), :]
        m_scr[...] = jnp.full(m_scr.shape, NEG_INF, F32)
        l_scr[...] = jnp.zeros(l_scr.shape, F32)
        acc_scr[...] = jnp.zeros(acc_scr.shape, F32)
        first_chunk = (q_pos0 + qi * tq) // CHUNK
        last_chunk = (q_pos0 + qi * tq + tq - 1) // CHUNK
        n_full = jnp.minimum(((first_chunk + 1) * CHUNK) // tk, nk_total)
        n_need = jnp.minimum(((last_chunk + 1) * CHUNK + tk - 1) // tk, nk_total)

        def k_block(kj, masked):
            c0 = pl.multiple_of(kj * tk, tk)
            kcat = jnp.concatenate([k_ref[pl.ds(c0, tk), :], kr_ref[pl.ds(c0, tk), :]], axis=1)
            s = lax.dot_general(q, kcat, (((1,), (1,)), ((), ())), preferred_element_type=F32)
            if masked:
                qc = (q_pos0 + r0 + lax.broadcasted_iota(jnp.int32, (tq, tk), 0)) // CHUNK
                kc = (c0 + lax.broadcasted_iota(jnp.int32, (tq, tk), 1)) // CHUNK
                s = jnp.where(kc <= qc, s, NEG_INF)
            m_prev = m_scr[...]
            m_new = jnp.maximum(m_prev, jnp.max(s, axis=1, keepdims=True))
            alpha = jnp.exp(m_prev - m_new)
            p = jnp.exp(s - m_new[:, :1])
            l_scr[...] = alpha * l_scr[...] + jnp.sum(p, axis=1, keepdims=True)
            acc_scr[...] = alpha[:, :acc_scr.shape[1]] * acc_scr[...] + jnp.dot(
                p.astype(BF16), v_ref[pl.ds(c0, tk), :], preferred_element_type=F32)
            m_scr[...] = m_new

        lax.fori_loop(0, n_full, lambda kj, c: (k_block(kj, False), c)[1], 0)
        lax.fori_loop(n_full, n_need, lambda kj, c: (k_block(kj, True), c)[1], 0)
        o_ref[pl.ds(r0, tq), :] = (acc_scr[...] / l_scr[:, :acc_scr.shape[1]]).astype(o_ref.dtype)
        return carry

    lax.fori_loop(0, nq, q_block, 0)


def _attention(q, q_blk0, kv, kr, n_streams, n_heads, lq, lk, q_pos0):
    tq = _divisor_tile(lq, 256, 16)
    tk = _divisor_tile(lk, 512, 16)
    kern = functools.partial(_attn_kernel, lq=lq, lk=lk, tq=tq, tk=tk, q_pos0=q_pos0)
    return pl.pallas_call(
        kern,
        out_shape=jax.ShapeDtypeStruct((n_streams * lq, n_heads * LANES), BF16),
        grid=(n_streams, n_heads),
        in_specs=[pl.BlockSpec((lq, 2 * LANES), lambda b, h: (q_blk0 + b, h)),
                  pl.BlockSpec((lk, LANES), lambda b, h: (b, h)),
                  pl.BlockSpec((lk, LANES), lambda b, h: (b, 0)),
                  pl.BlockSpec((lk, LANES), lambda b, h: (b, n_heads + h))],
        out_specs=pl.BlockSpec((lq, LANES), lambda b, h: (b, h)),
        scratch_shapes=[pltpu.VMEM((tq, LANES), F32), pltpu.VMEM((tq, LANES), F32),
                        pltpu.VMEM((tq, LANES), F32)],
        compiler_params=_params(("parallel", "parallel")),
        name="attention",
    )(q, kv, kr, kv)


CONV_W = 4


def _rglru_kernel(x_ref, gy_ref, past_ref, h0_ref, cw_ref, cb_ref, wra_ref, bra_ref, wri_ref, bri_ref, lam_ref,
                  hg_ref, hl_ref, xs_scr, h_scr, *, tt, reset_first):
    t = pl.program_id(2)
    nt = pl.num_programs(2)

    @pl.when(t == 0)
    def _():
        xs_scr[0:SUBLANES, :] = past_ref[...]
        h_scr[...] = h0_ref[...]

    x = x_ref[...]
    xs_scr[SUBLANES:, :] = x
    xc = cb_ref[...]
    for j in range(CONV_W):
        start = SUBLANES - (CONV_W - 1) + j
        xc = xc + xs_scr[start:start + tt, :] * cw_ref[j:j + 1, :]
    xs_scr[0:SUBLANES, :] = x[tt - SUBLANES:, :]

    xcb = xc.astype(BF16)
    r = jax.nn.sigmoid(jnp.dot(xcb, wra_ref[...], preferred_element_type=F32) + bra_ref[...])
    i = jax.nn.sigmoid(jnp.dot(xcb, wri_ref[...], preferred_element_type=F32) + bri_ref[...])
    nl = -lam_ref[...]
    softplus = jnp.maximum(nl, 0.0) + jnp.log1p(jnp.exp(-jnp.abs(nl)))
    a = jnp.exp(-LRU_C * r * softplus)
    mult = jnp.sqrt(1.0 - a * a)
    row = lax.broadcasted_iota(jnp.int32, a.shape, 0)
    if reset_first:
        first = jnp.logical_and(row == 0, t == 0)
        a = jnp.where(first, 0.0, a)
        mult = jnp.where(first, 1.0, mult)
    b = mult * i * xc

    d = 1
    while d < tt:
        valid = row >= d
        b = jnp.where(valid, a * pltpu.roll(b, d, axis=0) + b, b)
        a = jnp.where(valid, a * pltpu.roll(a, d, axis=0), a)
        d *= 2
    h = a * h_scr[...] + b
    h_scr[...] = h[tt - 1:tt, :]
    hg_ref[...] = (h * gy_ref[...].astype(F32)).astype(hg_ref.dtype)

    @pl.when(t == nt - 1)
    def _():
        hl_ref[...] = h[tt - 1:tt, :]


def _rglru(xb, gy, row0, n_streams, length, conv_past, h0, conv_w, conv_b, w_ra, b_ra, w_ri, b_ri, lam, reset_first):
    c = xb.shape[1]
    nb, bw, _ = w_ra.shape
    tt = _divisor_tile(length, 256, SUBLANES)
    assert tt >= SUBLANES and (tt & (tt - 1)) == 0 and row0 % tt == 0
    nt = length // tt
    blk0 = row0 // tt
    past8 = jnp.concatenate(
        [jnp.zeros((n_streams, SUBLANES - (CONV_W - 1), c), F32), conv_past.astype(F32)], axis=1)
    vec = lambda v: v.reshape(1, c).astype(F32)
    rows = lambda b, j, t: (blk0 + b * nt + t, j)
    chan = lambda b, j, t: (0, j)
    state = lambda b, j, t: (b, 0, j)
    wblk = lambda b, j, t: (j, 0, 0)
    hg, h_last = pl.pallas_call(
        functools.partial(_rglru_kernel, tt=tt, reset_first=reset_first),
        out_shape=(jax.ShapeDtypeStruct((n_streams * length, c), BF16),
                   jax.ShapeDtypeStruct((n_streams, 1, c), F32)),
        grid=(n_streams, nb, nt),
        in_specs=[pl.BlockSpec((tt, bw), rows), pl.BlockSpec((tt, bw), rows),
                  pl.BlockSpec((None, SUBLANES, bw), state), pl.BlockSpec((None, 1, bw), state),
                  pl.BlockSpec((CONV_W, bw), chan), pl.BlockSpec((1, bw), chan),
                  pl.BlockSpec((None, bw, bw), wblk), pl.BlockSpec((1, bw), chan),
                  pl.BlockSpec((None, bw, bw), wblk), pl.BlockSpec((1, bw), chan),
                  pl.BlockSpec((1, bw), chan)],
        out_specs=(pl.BlockSpec((tt, bw), lambda b, j, t: (b * nt + t, j)),
                   pl.BlockSpec((None, 1, bw), state)),
        scratch_shapes=[pltpu.VMEM((tt + SUBLANES, bw), F32), pltpu.VMEM((1, bw), F32)],
        compiler_params=_params(("parallel", "parallel", "arbitrary")),
        name="rglru",
    )(xb, gy, past8, h0.reshape(n_streams, 1, c).astype(F32), conv_w.astype(F32), vec(conv_b),
      w_ra.astype(BF16), vec(b_ra), w_ri.astype(BF16), vec(b_ri), vec(lam))
    return hg, h_last.reshape(n_streams, c)


def _router_kernel(h_ref, g_ref, wr_ref, br_ref, topi_ref, prob_ref, rank_ref, cnt_ref, carry_scr, *, n_exp, tm):
    step = pl.program_id(0)

    @pl.when(step == 0)
    def _():
        carry_scr[...] = jnp.zeros(carry_scr.shape, F32)

    hn = _rms(h_ref[...], g_ref[...])
    logits = lax.dot_general(wr_ref[...], hn, (((1,), (1,)), ((), ())),
                             precision=lax.Precision.HIGHEST, preferred_element_type=F32) + br_ref[...]
    eidx = lax.broadcasted_iota(jnp.int32, (n_exp, tm), 0)
    vals = logits
    top_v, top_i = [], []
    for _ in range(TOP_K):
        m = jnp.max(vals, axis=0, keepdims=True)
        sel = jnp.min(jnp.where(vals == m, eidx, n_exp), axis=0, keepdims=True)
        top_v.append(m)
        top_i.append(sel)
        vals = jnp.where(eidx == sel, -jnp.inf, vals)
    ex = [jnp.exp(v - top_v[0]) for v in top_v]
    denom = ex[0]
    for e in ex[1:]:
        denom = denom + e
    onehot = jnp.zeros((n_exp, tm), F32)
    for sel in top_i:
        onehot = onehot + (eidx == sel).astype(F32)
    upper = (lax.broadcasted_iota(jnp.int32, (tm, tm), 0) < lax.broadcasted_iota(jnp.int32, (tm, tm), 1))
    before = jnp.dot(onehot.astype(BF16), upper.astype(BF16), preferred_element_type=F32) + carry_scr[...]
    for k in range(TOP_K):
        topi_ref[k:k + 1, :] = top_i[k]
        prob_ref[k:k + 1, :] = ex[k] / denom
        rank_ref[k:k + 1, :] = jnp.sum(jnp.where(eidx == top_i[k], before, 0.0), axis=0,
                                       keepdims=True).astype(jnp.int32)
    carry_scr[...] = carry_scr[...] + jnp.sum(onehot, axis=1, keepdims=True)
    cnt_ref[...] = carry_scr[...].astype(jnp.int32)


def _router(h1, g_ffn, w_router, b_router):
    t, d = h1.shape
    n_exp = w_router.shape[1]
    tm = _divisor_tile(t, 512, LANES)
    kt = lambda i: (0, i)
    fixed = lambda i: (0, 0)
    return pl.pallas_call(
        functools.partial(_router_kernel, n_exp=n_exp, tm=tm),
        out_shape=(jax.ShapeDtypeStruct((TOP_K, t), jnp.int32), jax.ShapeDtypeStruct((TOP_K, t), F32),
                   jax.ShapeDtypeStruct((TOP_K, t), jnp.int32), jax.ShapeDtypeStruct((n_exp, 1), jnp.int32)),
        grid=(t // tm,),
        in_specs=[pl.BlockSpec((tm, d), lambda i: (i, 0)), pl.BlockSpec((1, d), fixed),
                  pl.BlockSpec((n_exp, d), fixed), pl.BlockSpec((n_exp, 1), fixed)],
        out_specs=(pl.BlockSpec((TOP_K, tm), kt), pl.BlockSpec((TOP_K, tm), kt), pl.BlockSpec((TOP_K, tm), kt),
                   pl.BlockSpec((n_exp, 1), fixed)),
        scratch_shapes=[pltpu.VMEM((n_exp, 1), F32)],
        compiler_params=_params(("arbitrary",)),
        name="router",
    )(h1, g_ffn.reshape(1, d), w_router.T.astype(F32), b_router.reshape(n_exp, 1).astype(F32))


def _gather_norm_kernel(src_ref, h_hbm, g_ref, o_ref, buf, sem, *, rows):
    def row_copy(r):
        return pltpu.make_async_copy(h_hbm.at[pl.ds(src_ref[0, r], 1), :], buf.at[pl.ds(r, 1), :], sem)

    def start(r, c):
        row_copy(r).start()
        return c

    def wait(r, c):
        row_copy(r).wait()
        return c

    lax.fori_loop(0, rows, start, 0)
    lax.fori_loop(0, rows, wait, 0)
    o_ref[...] = _rms(buf[...], g_ref[...]).astype(o_ref.dtype)


def _gather_norm(h1, g_ffn, src_token, rows):
    t, d = h1.shape
    n_rows = src_token.shape[0]
    return pl.pallas_call(
        functools.partial(_gather_norm_kernel, rows=rows),
        out_shape=jax.ShapeDtypeStruct((n_rows, d), BF16),
        grid=(n_rows // rows,),
        in_specs=[pl.BlockSpec((None, 1, rows), lambda i: (i, 0, 0), memory_space=pltpu.SMEM),
                  pl.BlockSpec(memory_space=pl.ANY), pl.BlockSpec((1, d), lambda i: (0, 0))],
        out_specs=pl.BlockSpec((rows, d), lambda i: (i, 0)),
        scratch_shapes=[pltpu.VMEM((rows, d), F32), pltpu.SemaphoreType.DMA(())],
        compiler_params=_params(("arbitrary",)),
        name="gather_norm",
    )(src_token.reshape(n_rows // rows, 1, rows), h1, g_ffn.reshape(1, d))


def _expert_up_kernel(te_ref, nu_ref, x_ref, wg_ref, bg_ref, wu_ref, bu_ref, o_ref):
    m = pl.program_id(1)

    @pl.when(m < nu_ref[0])
    def _():
        x = x_ref[...]
        g = jnp.dot(x, wg_ref[...].astype(BF16), preferred_element_type=F32) + bg_ref[...]
        u = jnp.dot(x, wu_ref[...].astype(BF16), preferred_element_type=F32) + bu_ref[...]
        g = jnp.minimum(g, SWIGLU_LIMIT)
        u = jnp.clip(u, -SWIGLU_LIMIT, SWIGLU_LIMIT)
        o_ref[...] = ((u + 1.0) * (g * jax.nn.sigmoid(SWIGLU_ALPHA * g))).astype(o_ref.dtype)

    @pl.when(m >= nu_ref[0])
    def _():
        o_ref[...] = jnp.zeros(o_ref.shape, o_ref.dtype)


def _expert_down_kernel(te_ref, nu_ref, x_ref, wd_ref, bd_ref, o_ref):
    m = pl.program_id(1)

    @pl.when(m < nu_ref[0])
    def _():
        o_ref[...] = jnp.dot(x_ref[...], wd_ref[...].astype(BF16), preferred_element_type=F32) + bd_ref[...]

    @pl.when(m >= nu_ref[0])
    def _():
        o_ref[...] = jnp.zeros(o_ref.shape, o_ref.dtype)


def _expert_up(xbuf, tile_expert, n_used, w_gate, b_gate, w_up, b_up, tm, tn):
    n_rows, d = xbuf.shape
    n_exp, _, d_ff = w_gate.shape
    tn = _divisor_tile(d_ff, tn, 2 * LANES)
    wmap = lambda j, i, te, nu: (te[i], 0, j)
    bmap = lambda j, i, te, nu: (te[i], 0, j)
    return pl.pallas_call(
        _expert_up_kernel,
        out_shape=jax.ShapeDtypeStruct((n_rows, d_ff), BF16),
        grid_spec=pltpu.PrefetchScalarGridSpec(
            num_scalar_prefetch=2,
            grid=(d_ff // tn, n_rows // tm),
            in_specs=[pl.BlockSpec((tm, d), lambda j, i, te, nu: (i, 0)),
                      pl.BlockSpec((None, d, tn), wmap), pl.BlockSpec((None, 1, tn), bmap),
                      pl.BlockSpec((None, d, tn), wmap), pl.BlockSpec((None, 1, tn), bmap)],
            out_specs=pl.BlockSpec((tm, tn), lambda j, i, te, nu: (i, j))),
        compiler_params=_params(("parallel", "arbitrary")),
        name="expert_up",
    )(tile_expert, n_used, xbuf, w_gate, b_gate.reshape(n_exp, 1, d_ff), w_up, b_up.reshape(n_exp, 1, d_ff))


def _expert_down(hbuf, tile_expert, n_used, w_down, b_down, tm, tn):
    n_rows, d_ff = hbuf.shape
    n_exp, _, d = w_down.shape
    tn = _divisor_tile(d, tn, 2 * LANES)
    return pl.pallas_call(
        _expert_down_kernel,
        out_shape=jax.ShapeDtypeStruct((n_rows, d), F32),
        grid_spec=pltpu.PrefetchScalarGridSpec(
            num_scalar_prefetch=2,
            grid=(d // tn, n_rows // tm),
            in_specs=[pl.BlockSpec((tm, d_ff), lambda j, i, te, nu: (i, 0)),
                      pl.BlockSpec((None, d_ff, tn), lambda j, i, te, nu: (te[i], 0, j)),
                      pl.BlockSpec((None, 1, tn), lambda j, i, te, nu: (te[i], 0, j))],
            out_specs=pl.BlockSpec((tm, tn), lambda j, i, te, nu: (i, j))),
        compiler_params=_params(("parallel", "arbitrary")),
        name="expert_down",
    )(tile_expert, n_used, hbuf, w_down, b_down.reshape(n_exp, 1, d))


def _combine_kernel(dest_ref, h_ref, p_ref, y_hbm, g_ref, o_ref, buf, sem, *, tm, final_norm):
    def row_copy(k, r):
        return pltpu.make_async_copy(y_hbm.at[pl.ds(dest_ref[k, r], 1), :], buf.at[k, pl.ds(r, 1), :], sem)

    def start(r, c):
        for k in range(TOP_K):
            row_copy(k, r).start()
        return c

    def wait(r, c):
        for k in range(TOP_K):
            row_copy(k, r).wait()
        return c

    lax.fori_loop(0, tm, start, 0)
    lax.fori_loop(0, tm, wait, 0)
    acc = jnp.zeros(h_ref.shape, F32)
    for k in range(TOP_K):
        acc = acc + buf[k] * p_ref[:, k:k + 1]
    out = h_ref[...] + acc
    o_ref[...] = _rms(out, g_ref[...]) if final_norm else out


def _combine(h1, probs_kt, dest_kt, ybuf, g_final, final_norm, tm=128):
    t, d = h1.shape
    tm = _divisor_tile(t, tm, SUBLANES)
    nt = t // tm
    dest_tiles = dest_kt.reshape(TOP_K, nt, tm).transpose(1, 0, 2)
    return pl.pallas_call(
        functools.partial(_combine_kernel, tm=tm, final_norm=final_norm),
        out_shape=jax.ShapeDtypeStruct((t, d), F32),
        grid=(nt,),
        in_specs=[pl.BlockSpec((None, TOP_K, tm), lambda i: (i, 0, 0), memory_space=pltpu.SMEM),
                  pl.BlockSpec((tm, d), lambda i: (i, 0)), pl.BlockSpec((tm, TOP_K), lambda i: (i, 0)),
                  pl.BlockSpec(memory_space=pl.ANY), pl.BlockSpec((1, d), lambda i: (0, 0))],
        out_specs=pl.BlockSpec((tm, d), lambda i: (i, 0)),
        scratch_shapes=[pltpu.VMEM((TOP_K, tm, d), F32), pltpu.SemaphoreType.DMA(())],
        compiler_params=_params(("arbitrary",)),
        name="combine",
    )(dest_tiles, h1, probs_kt.T, ybuf, g_final.reshape(1, d))


EXPERT_ROW_TILE = 256
EXPERT_COL_TILE = 512


def _rope_tables(pos, rope_dim):
    half = rope_dim // 2
    inv = ROPE_THETA ** (-jnp.arange(half, dtype=F32) / half)
    ang = pos.astype(F32)[:, None] * inv[None, :]
    cos, sin = jnp.cos(ang), jnp.sin(ang)
    zeros = jnp.zeros((pos.shape[0], LANES - rope_dim), F32)
    return jnp.concatenate([cos, cos, zeros], axis=1), jnp.concatenate([-sin, sin, zeros], axis=1)


def _swap_halves(w):
    half = w.shape[-1] // 2
    return jnp.concatenate([w[..., half:], w[..., :half]], axis=-1)


def _moe(h1, g_ffn, w_router, b_router, w_gate, b_gate, w_up, b_up, w_down, b_down, g_out, final_norm):
    t, d = h1.shape
    n_exp = w_router.shape[1]
    tme = EXPERT_ROW_TILE
    top_i, probs, rank, counts = _router(h1, g_ffn, w_router, b_router)
    counts = counts.reshape(n_exp)
    padded = (counts + tme - 1) // tme * tme
    pend = jnp.cumsum(padded)
    pstart = pend - padded
    dest = pstart[top_i] + rank
    n_assign = t * TOP_K
    n_rows = (n_assign + tme - 1) // tme * tme + n_exp * tme
    tile_start = jnp.arange(n_rows // tme, dtype=jnp.int32) * tme
    tile_expert = jnp.minimum(jnp.searchsorted(pend, tile_start, side="right"), n_exp - 1).astype(jnp.int32)
    n_used = (pend[-1] // tme).astype(jnp.int32).reshape(1)
    tok = jnp.tile(jnp.arange(t, dtype=jnp.int32), TOP_K)
    src_token = jnp.zeros((n_rows,), jnp.int32).at[dest.reshape(-1)].set(tok)
    xbuf = _gather_norm(h1, g_ffn, src_token, tme)
    hbuf = _expert_up(xbuf, tile_expert, n_used, w_gate, b_gate, w_up, b_up, tme, EXPERT_COL_TILE)
    ybuf = _expert_down(hbuf, tile_expert, n_used, w_down, b_down, tme, EXPERT_COL_TILE)
    return _combine(h1, probs, dest, ybuf, g_out, final_norm)


def _layer(x2d, geom, cc, ss, past_ckv, past_krope, conv_s, lru_s, lw, g_out, final_norm):
    (g_mix, w_in, g_q, w_uq, g_kv, w_uk, w_uv, w_o_attn, conv_w, conv_b, w_ra, b_ra, w_ri, b_ri,
     lam, w_o_rnn, w_out, g_ffn, w_router, b_router, w_gate, b_gate, w_up, b_up, w_down, b_down) = lw
    bp, lp, bs, ls, past = geom
    tp = bp * lp
    t, d = x2d.shape
    q_lora, n_heads, qk_dim = w_uq.shape
    kv_lora, _, nope = w_uk.shape
    v_dim = w_uv.shape[2]
    rope = qk_dim - nope
    c = conv_w.shape[1]
    assert nope == LANES and v_dim == LANES and 2 * rope == LANES and tp % ls == 0

    o1, o2, o3, o4 = q_lora + kv_lora, q_lora + kv_lora + rope, q_lora + kv_lora + rope + c, q_lora + kv_lora + rope + 2 * c
    w_kr = w_in[:, o1:o2]
    w_head = jnp.concatenate([w_in[:, :o1], w_kr, _swap_halves(w_kr)], axis=1).astype(BF16)
    w_xb = w_in[:, o2:o3].astype(BF16)
    w_yb = w_in[:, o3:o4].astype(BF16)
    w_mg = w_in[:, o4:].astype(BF16)
    w_qr = w_uq[:, :, nope:]
    w_q = jnp.concatenate([w_uq[:, :, :nope], w_qr, _swap_halves(w_qr)], axis=2).reshape(q_lora, -1).astype(BF16)
    w_ukv = jnp.concatenate([w_uk.reshape(kv_lora, -1), w_uv.reshape(kv_lora, -1)], axis=1).astype(BF16)

    xn = _rmsnorm_bf16(x2d, g_mix)
    cqn, ckv, ckv_b, kr, kr_b = _head_proj(xn, w_head, g_q, g_kv, cc, ss)
    xb = _matmul([(xn, w_xb)], _ep_identity, F32, name="proj_xb")
    gy = _matmul([(xn, w_yb)], _ep_gelu, BF16, name="proj_yb")
    gates = _matmul([(xn, w_mg)], _ep_sigmoid, BF16, name="proj_gates")

    scale = float(qk_dim) ** -0.5
    q = _matmul([(cqn, w_q)], functools.partial(_ep_q_rope, scale=scale), BF16, row_extras=(cc, ss), name="q_up")
    kv_p = _matmul([(ckv_b[:tp], w_ukv)], _ep_identity, BF16, name="kv_up_prompt")
    o_p = _attention(q, 0, kv_p, kr_b[:tp], bp, n_heads, lp, lp, 0)
    ckv_all = jnp.concatenate([past_ckv.astype(BF16), ckv_b[tp:].reshape(bs, ls, kv_lora)], axis=1)
    kr_past = jnp.pad(past_krope, ((0, 0), (0, 0), (0, LANES - rope))).astype(BF16)
    kr_all = jnp.concatenate([kr_past, kr_b[tp:].reshape(bs, ls, LANES)], axis=1).reshape(bs * (past + ls), LANES)
    kv_s = _matmul([(ckv_all.reshape(bs * (past + ls), kv_lora), w_ukv)], _ep_identity, BF16, name="kv_up_sample")
    o_s = _attention(q, tp // ls, kv_s, kr_all, bs, n_heads, ls, past + ls, past)
    o = jnp.concatenate([o_p, o_s], axis=0)

    hg_p, lru_p = _rglru(xb, gy, 0, bp, lp, jnp.zeros((bp, CONV_W - 1, c), F32), jnp.zeros((bp, c), F32),
                         conv_w, conv_b, w_ra, b_ra, w_ri, b_ri, lam, True)
    hg_s, lru_s_new = _rglru(xb, gy, tp, bs, ls, conv_s, lru_s, conv_w, conv_b, w_ra, b_ra, w_ri, b_ri, lam, past == 0)
    hg = jnp.concatenate([hg_p, hg_s], axis=0)
    xpad_p = jnp.concatenate([jnp.zeros((bp, CONV_W - 1, c), F32), xb[:tp].reshape(bp, lp, c)], axis=1)
    xpad_s = jnp.concatenate([conv_s.astype(F32), xb[tp:].reshape(bs, ls, c)], axis=1)
    conv_p_new, conv_s_new = xpad_p[:, -(CONV_W - 1):], xpad_s[:, -(CONV_W - 1):]

    mix = _matmul([(o, w_o_attn.astype(BF16)), (hg, w_o_rnn.astype(BF16))], _ep_merge, BF16,
                  tile_extras=((gates, 0), (gates, d)), tn=512, name="merge")
    h1 = _matmul([(mix, w_out.astype(BF16))], _ep_residual, F32, tile_extras=((x2d, 0),), name="out_proj")

    h2 = _moe(h1, g_ffn, w_router, b_router, w_gate, b_gate, w_up, b_up, w_down, b_down, g_out, final_norm)
    states_p = (ckv[:tp].reshape(bp, lp, kv_lora), kr[:tp, :rope].reshape(bp, lp, rope), conv_p_new, lru_p)
    states_s = (ckv[tp:].reshape(bs, ls, kv_lora), kr[tp:, :rope].reshape(bs, ls, rope), conv_s_new, lru_s_new)
    return h2, states_p, states_s


def kernel(x_prompt, x_sample, cache_ckv, cache_krope, state_conv, state_lru, g_mix, w_in, g_q, w_uq, g_kv, w_uk, w_uv, w_o_attn, conv_w, conv_b, w_ra, b_ra, w_ri, b_ri, lam, w_o_rnn, w_out, g_ffn, w_router, b_router, w_gate, b_gate, w_up, b_up, w_down, b_down, g_final):
    bp, lp, d = x_prompt.shape
    bs, ls, _ = x_sample.shape
    depth, _, past, _ = cache_ckv.shape
    rope = cache_krope.shape[-1]
    geom = (bp, lp, bs, ls, past)
    pos = jnp.concatenate([jnp.tile(jnp.arange(lp, dtype=jnp.int32), bp),
                           jnp.tile(past + jnp.arange(ls, dtype=jnp.int32), bs)])
    cc, ss = _rope_tables(pos, rope)
    h = jnp.concatenate([x_prompt.reshape(bp * lp, d), x_sample.reshape(bs * ls, d)], axis=0)
    weights = (g_mix, w_in, g_q, w_uq, g_kv, w_uk, w_uv, w_o_attn, conv_w, conv_b, w_ra, b_ra, w_ri, b_ri,
               lam, w_o_rnn, w_out, g_ffn, w_router, b_router, w_gate, b_gate, w_up, b_up, w_down, b_down)
    st_p, st_s = [], []
    for layer in range(depth):
        lw = tuple(w[layer] for w in weights)
        last = layer == depth - 1
        h, sp, ss_ = _layer(h, geom, cc, ss, cache_ckv[layer], cache_krope[layer], state_conv[layer],
                            state_lru[layer], lw, g_final if last else g_mix[layer], last)
        st_p.append(sp)
        st_s.append(ss_)
    tp = bp * lp
    outs = [h[:tp].reshape(bp, lp, d), h[tp:].reshape(bs, ls, d)]
    for group in (st_p, st_s):
        for k in range(4):
            outs.append(jnp.stack([s[k] for s in group]))
    return tuple(outs)
```

```python
import functools
import math

import jax
import jax.numpy as jnp
import numpy as np
from jax import lax
from jax.experimental import pallas as pl
from jax.experimental.pallas import tpu as pltpu

CHUNK = 64
ROPE_THETA = 10000.0
LRU_C = 8.0
TOP_K = 4
SWIGLU_LIMIT = 7.0
SWIGLU_ALPHA = 1.702
NORM_EPS = 1e-6
NEG_INF = -1e30

LANES = 128
SUBLANES = 8
V7X_VMEM_LIMIT_BYTES = 56 * 1024 * 1024

BF16 = jnp.bfloat16
F32 = jnp.float32


def _divisor_tile(n, pref, mult):
    if n <= pref:
        return n
    t = (pref // mult) * mult
    while t >= mult:
        if n % t == 0:
            return t
        t -= mult
    return n


def _params(sem):
    return pltpu.CompilerParams(dimension_semantics=sem, vmem_limit_bytes=V7X_VMEM_LIMIT_BYTES)


def _rms(x, g):
    return x * lax.rsqrt(jnp.mean(x * x, axis=-1, keepdims=True) + NORM_EPS) * g


def _rmsnorm_kernel(x_ref, g_ref, o_ref):
    o_ref[...] = _rms(x_ref[...], g_ref[...]).astype(o_ref.dtype)


def _rmsnorm_bf16(x, g):
    t, d = x.shape
    tm = _divisor_tile(t, 256, SUBLANES)
    return pl.pallas_call(
        _rmsnorm_kernel,
        out_shape=jax.ShapeDtypeStruct((t, d), BF16),
        grid=(t // tm,),
        in_specs=[pl.BlockSpec((tm, d), lambda i: (i, 0)), pl.BlockSpec((1, d), lambda i: (0, 0))],
        out_specs=pl.BlockSpec((tm, d), lambda i: (i, 0)),
        compiler_params=_params(("parallel",)),
        name="rmsnorm",
    )(x, g.reshape(1, d))


def _matmul_kernel(*refs, n_pairs, n_tile, n_row, epilogue):
    a_refs = refs[0:2 * n_pairs:2]
    b_refs = refs[1:2 * n_pairs:2]
    tile_refs = refs[2 * n_pairs:2 * n_pairs + n_tile]
    row_refs = refs[2 * n_pairs + n_tile:2 * n_pairs + n_tile + n_row]
    o_ref = refs[-1]
    accs = [jnp.dot(a[...], b[...], preferred_element_type=F32) for a, b in zip(a_refs, b_refs)]
    o_ref[...] = epilogue(accs, [r[...] for r in tile_refs], [r[...] for r in row_refs]).astype(o_ref.dtype)


def _matmul(pairs, epilogue, out_dtype, tile_extras=(), row_extras=(), tm=512, tn=1024, name="matmul"):
    m = pairs[0][0].shape[0]
    n = pairs[0][1].shape[1]
    tm = _divisor_tile(m, tm, 16)
    tn = _divisor_tile(n, tn, 2 * LANES)
    in_specs, args = [], []
    for a, b in pairs:
        in_specs += [pl.BlockSpec((tm, a.shape[1]), lambda j, i: (i, 0)),
                     pl.BlockSpec((b.shape[0], tn), lambda j, i: (0, j))]
        args += [a, b]
    for e, col0 in tile_extras:
        assert col0 % tn == 0
        in_specs.append(pl.BlockSpec((tm, tn), functools.partial(lambda j, i, jb: (i, j + jb), jb=col0 // tn)))
        args.append(e)
    for e in row_extras:
        in_specs.append(pl.BlockSpec((tm, e.shape[1]), lambda j, i: (i, 0)))
        args.append(e)
    kern = functools.partial(_matmul_kernel, n_pairs=len(pairs), n_tile=len(tile_extras),
                             n_row=len(row_extras), epilogue=epilogue)
    return pl.pallas_call(
        kern,
        out_shape=jax.ShapeDtypeStruct((m, n), out_dtype),
        grid=(n // tn, m // tm),
        in_specs=in_specs,
        out_specs=pl.BlockSpec((tm, tn), lambda j, i: (i, j)),
        compiler_params=_params(("parallel", "parallel")),
        name=name,
    )(*args)


def _ep_identity(accs, tiles, rows):
    return accs[0]


def _ep_gelu(accs, tiles, rows):
    return jax.nn.gelu(accs[0])


def _ep_sigmoid(accs, tiles, rows):
    return jax.nn.sigmoid(accs[0])


def _rope_block(blk, cc, ss):
    return blk * cc + pltpu.roll(blk, LANES // 2, axis=1) * ss


def _ep_q_rope(accs, tiles, rows, *, scale):
    acc = accs[0]
    cc, ss = rows
    out = []
    for h in range(acc.shape[1] // (2 * LANES)):
        base = h * 2 * LANES
        out.append(acc[:, base:base + LANES] * scale)
        out.append(_rope_block(acc[:, base + LANES:base + 2 * LANES], cc, ss) * scale)
    return jnp.concatenate(out, axis=1)


def _ep_merge(accs, tiles, rows):
    return tiles[0].astype(F32) * accs[0] + tiles[1].astype(F32) * accs[1]


def _ep_residual(accs, tiles, rows):
    return tiles[0] + accs[0]


def _head_kernel(x_ref, w_ref, gq_ref, gkv_ref, cc_ref, ss_ref,
                 cqn_ref, ckv_ref, ckvb_ref, kr_ref, krb_ref, *, q_lora, kv_lora):
    p = jnp.dot(x_ref[...], w_ref[...], preferred_element_type=F32)
    cqn_ref[...] = _rms(p[:, :q_lora], gq_ref[...]).astype(cqn_ref.dtype)
    ckv = _rms(p[:, q_lora:q_lora + kv_lora], gkv_ref[...])
    ckv_ref[...] = ckv
    ckvb_ref[...] = ckv.astype(ckvb_ref.dtype)
    kr = _rope_block(p[:, q_lora + kv_lora:], cc_ref[...], ss_ref[...])
    kr_ref[...] = kr
    krb_ref[...] = kr.astype(krb_ref.dtype)


def _head_proj(xn, w_head, g_q, g_kv, cc, ss):
    t, d = xn.shape
    q_lora, kv_lora = g_q.shape[0], g_kv.shape[0]
    nh = w_head.shape[1]
    tm = _divisor_tile(t, 512, 16)
    row = lambda i: (i, 0)
    fixed = lambda i: (0, 0)
    return pl.pallas_call(
        functools.partial(_head_kernel, q_lora=q_lora, kv_lora=kv_lora),
        out_shape=(jax.ShapeDtypeStruct((t, q_lora), BF16),
                   jax.ShapeDtypeStruct((t, kv_lora), F32),
                   jax.ShapeDtypeStruct((t, kv_lora), BF16),
                   jax.ShapeDtypeStruct((t, LANES), F32),
                   jax.ShapeDtypeStruct((t, LANES), BF16)),
        grid=(t // tm,),
        in_specs=[pl.BlockSpec((tm, d), row), pl.BlockSpec((d, nh), fixed),
                  pl.BlockSpec((1, q_lora), fixed), pl.BlockSpec((1, kv_lora), fixed),
                  pl.BlockSpec((tm, LANES), row), pl.BlockSpec((tm, LANES), row)],
        out_specs=(pl.BlockSpec((tm, q_lora), row), pl.BlockSpec((tm, kv_lora), row),
                   pl.BlockSpec((tm, kv_lora), row), pl.BlockSpec((tm, LANES), row),
                   pl.BlockSpec((tm, LANES), row)),
        compiler_params=_params(("parallel",)),
        name="head_proj",
    )(xn, w_head, g_q.reshape(1, -1), g_kv.reshape(1, -1), cc, ss)


def _key_block_range(q_pos_first, q_pos_last, tk, nk_total):
    n_full = jnp.minimum(((q_pos_first // CHUNK + 1) * CHUNK) // tk, nk_total)
    n_need = jnp.minimum(((q_pos_last // CHUNK + 1) * CHUNK + tk - 1) // tk, nk_total)
    return n_full, n_need


def _softmax_step(s, v, m_ref, l_ref, acc_ref):
    n_s = s.shape[1] // LANES
    s_tiles = [s[:, c * LANES:(c + 1) * LANES] for c in range(n_s)]
    if s.shape[1] % LANES:
        tail = s[:, n_s * LANES:]
        s_tiles.append(jnp.concatenate([tail, jnp.full((s.shape[0], LANES - tail.shape[1]), NEG_INF, F32)], axis=1))
    tile_max = functools.reduce(jnp.maximum, s_tiles)
    m_prev = m_ref[...]
    m_new = jnp.maximum(m_prev, jnp.max(tile_max, axis=1, keepdims=True))
    alpha = jnp.exp2(m_prev - m_new)
    p_tiles = [jnp.exp2(t - m_new) for t in s_tiles]
    tile_sum = functools.reduce(jnp.add, p_tiles)
    l_ref[...] = alpha * l_ref[...] + jnp.sum(tile_sum, axis=1, keepdims=True)
    p = jnp.concatenate(p_tiles, axis=1)[:, :s.shape[1]].astype(BF16)
    pv = jnp.dot(p, v, preferred_element_type=F32)
    for c in range(acc_ref.shape[1] // LANES):
        cols = slice(c * LANES, (c + 1) * LANES)
        acc_ref[:, cols] = alpha * acc_ref[:, cols] + pv[:, cols]
    m_ref[...] = m_new


def _attn_kernel(q_ref, k_ref, kr_ref, v_ref, o_ref, m_scr, l_scr, acc_scr, *, hp, lq, lk, tq, tk, q_pos0):
    nq = lq // tq
    nk_total = lk // tk

    def q_block(qi, carry):
        r0 = pl.multiple_of(qi * tq, tq)
        m_scr[...] = jnp.full(m_scr.shape, NEG_INF, F32)
        l_scr[...] = jnp.zeros(l_scr.shape, F32)
        acc_scr[...] = jnp.zeros(acc_scr.shape, F32)
        n_full, n_need = _key_block_range(q_pos0 + r0, q_pos0 + r0 + tq - 1, tk, nk_total)

        def k_block(kj, masked):
            c0 = pl.multiple_of(kj * tk, tk)
            kr = kr_ref[pl.ds(c0, tk), :]
            if masked:
                qc = (q_pos0 + r0 + lax.broadcasted_iota(jnp.int32, (tq, tk), 0)) // CHUNK
                kc = (c0 + lax.broadcasted_iota(jnp.int32, (tq, tk), 1)) // CHUNK
                allowed = kc <= qc
            for h in range(hp):
                q = q_ref[pl.ds(r0, tq), h * 2 * LANES:(h + 1) * 2 * LANES]
                kcat = jnp.concatenate([k_ref[pl.ds(c0, tk), h * LANES:(h + 1) * LANES], kr], axis=1)
                s = lax.dot_general(q, kcat, (((1,), (1,)), ((), ())), preferred_element_type=F32)
                if masked:
                    s = jnp.where(allowed, s, NEG_INF)
                _softmax_step(s, v_ref[pl.ds(c0, tk), h * LANES:(h + 1) * LANES],
                              m_scr.at[h], l_scr.at[h], acc_scr.at[h])

        lax.fori_loop(0, n_full, lambda kj, c: (k_block(kj, False), c)[1], 0)
        lax.fori_loop(n_full, n_need, lambda kj, c: (k_block(kj, True), c)[1], 0)
        for h in range(hp):
            o_ref[pl.ds(r0, tq), h * LANES:(h + 1) * LANES] = (acc_scr[h] / l_scr[h]).astype(o_ref.dtype)
        return carry

    lax.fori_loop(0, nq, q_block, 0)


def _attention(q, q_blk0, kv, kr, n_streams, n_heads, lq, lk, q_pos0, hp=2):
    tq = _divisor_tile(lq, 512, 16)
    tk = _divisor_tile(lk, 512, 16)
    hp = hp if n_heads % hp == 0 else 1
    ng = n_heads // hp
    kern = functools.partial(_attn_kernel, hp=hp, lq=lq, lk=lk, tq=tq, tk=tk, q_pos0=q_pos0)
    return pl.pallas_call(
        kern,
        out_shape=jax.ShapeDtypeStruct((n_streams * lq, n_heads * LANES), BF16),
        grid=(n_streams, ng),
        in_specs=[pl.BlockSpec((lq, hp * 2 * LANES), lambda b, g: (q_blk0 + b, g)),
                  pl.BlockSpec((lk, hp * LANES), lambda b, g: (b, g)),
                  pl.BlockSpec((lk, LANES), lambda b, g: (b, 0)),
                  pl.BlockSpec((lk, hp * LANES), lambda b, g: (b, ng + g))],
        out_specs=pl.BlockSpec((lq, hp * LANES), lambda b, g: (b, g)),
        scratch_shapes=[pltpu.VMEM((hp, tq, LANES), F32), pltpu.VMEM((hp, tq, LANES), F32),
                        pltpu.VMEM((hp, tq, LANES), F32)],
        compiler_params=_params(("parallel", "parallel")),
        name="attention",
    )(q, kv, kr, kv)


def _attn_absorbed_kernel(q_ref, ckv_ref, kr_ref, wuk_ref, wuv_ref, o_ref, qcat_scr, m_scr, l_scr, acc_scr,
                          *, n_heads, lq, lk, tk, q_pos0):
    kvl = ckv_ref.shape[1]
    nk_total = lk // tk
    rows = n_heads * lq
    for h in range(n_heads):
        qn = q_ref[:, h * 2 * LANES:h * 2 * LANES + LANES]
        qa = lax.dot_general(qn, wuk_ref[:, h * LANES:(h + 1) * LANES], (((1,), (1,)), ((), ())),
                             preferred_element_type=F32)
        qcat_scr[h * lq:(h + 1) * lq, 0:kvl] = qa.astype(BF16)
        qcat_scr[h * lq:(h + 1) * lq, kvl:] = q_ref[:, h * 2 * LANES + LANES:(h + 1) * 2 * LANES]
    m_scr[...] = jnp.full(m_scr.shape, NEG_INF, F32)
    l_scr[...] = jnp.zeros(l_scr.shape, F32)
    acc_scr[...] = jnp.zeros(acc_scr.shape, F32)
    n_full, n_need = _key_block_range(q_pos0, q_pos0 + lq - 1, tk, nk_total)

    def k_block(kj, masked):
        c0 = pl.multiple_of(kj * tk, tk)
        ck = ckv_ref[pl.ds(c0, tk), :]
        kcat = jnp.concatenate([ck, kr_ref[pl.ds(c0, tk), :]], axis=1)
        s = lax.dot_general(qcat_scr[...], kcat, (((1,), (1,)), ((), ())), preferred_element_type=F32)
        if masked:
            qc = (q_pos0 + lax.broadcasted_iota(jnp.int32, (rows, tk), 0) % lq) // CHUNK
            kc = (c0 + lax.broadcasted_iota(jnp.int32, (rows, tk), 1)) // CHUNK
            s = jnp.where(kc <= qc, s, NEG_INF)
        _softmax_step(s, ck, m_scr, l_scr, acc_scr)

    lax.fori_loop(0, n_full, lambda kj, c: (k_block(kj, False), c)[1], 0)
    lax.fori_loop(n_full, n_need, lambda kj, c: (k_block(kj, True), c)[1], 0)
    for h in range(n_heads):
        inv_l = 1.0 / l_scr[h * lq:(h + 1) * lq, :]
        oh = jnp.concatenate([acc_scr[h * lq:(h + 1) * lq, c * LANES:(c + 1) * LANES] * inv_l
                              for c in range(kvl // LANES)], axis=1)
        o_ref[:, h * LANES:(h + 1) * LANES] = jnp.dot(
            oh.astype(BF16), wuv_ref[:, h * LANES:(h + 1) * LANES], preferred_element_type=F32).astype(o_ref.dtype)


def _attention_absorbed(q, q_blk0, ckv_all, kr_all, w_uk, w_uv, n_streams, n_heads, lq, lk, q_pos0):
    kvl = ckv_all.shape[1]
    tk = _divisor_tile(lk, 512, 16)
    rows = n_heads * lq
    kern = functools.partial(_attn_absorbed_kernel, n_heads=n_heads, lq=lq, lk=lk, tk=tk, q_pos0=q_pos0)
    fixed = lambda b: (0, 0)
    return pl.pallas_call(
        kern,
        out_shape=jax.ShapeDtypeStruct((n_streams * lq, n_heads * LANES), BF16),
        grid=(n_streams,),
        in_specs=[pl.BlockSpec((lq, n_heads * 2 * LANES), lambda b: (q_blk0 + b, 0)),
                  pl.BlockSpec((lk, kvl), lambda b: (b, 0)), pl.BlockSpec((lk, LANES), lambda b: (b, 0)),
                  pl.BlockSpec((kvl, n_heads * LANES), fixed), pl.BlockSpec((kvl, n_heads * LANES), fixed)],
        out_specs=pl.BlockSpec((lq, n_heads * LANES), lambda b: (b, 0)),
        scratch_shapes=[pltpu.VMEM((rows, kvl + LANES), BF16), pltpu.VMEM((rows, LANES), F32),
                        pltpu.VMEM((rows, LANES), F32), pltpu.VMEM((rows, kvl), F32)],
        compiler_params=_params(("parallel",)),
        name="attention_absorbed",
    )(q, ckv_all, kr_all, w_uk, w_uv)


CONV_W = 4


def _rglru_kernel(x_ref, gy_ref, past_ref, h0_ref, cw_ref, cb_ref, wra_ref, bra_ref, wri_ref, bri_ref, lam_ref,
                  hg_ref, hl_ref, xs_scr, h_scr, *, tt, reset_first):
    t = pl.program_id(2)
    nt = pl.num_programs(2)

    @pl.when(t == 0)
    def _():
        xs_scr[0:SUBLANES, :] = past_ref[...]
        h_scr[...] = h0_ref[...]

    x = x_ref[...]
    xs_scr[SUBLANES:, :] = x
    xc = cb_ref[...]
    for j in range(CONV_W):
        start = SUBLANES - (CONV_W - 1) + j
        xc = xc + xs_scr[start:start + tt, :] * cw_ref[j:j + 1, :]
    xs_scr[0:SUBLANES, :] = x[tt - SUBLANES:, :]

    xcb = xc.astype(BF16)
    r = jax.nn.sigmoid(jnp.dot(xcb, wra_ref[...], preferred_element_type=F32) + bra_ref[...])
    i = jax.nn.sigmoid(jnp.dot(xcb, wri_ref[...], preferred_element_type=F32) + bri_ref[...])
    nl = -lam_ref[...]
    softplus = jnp.maximum(nl, 0.0) + jnp.log1p(jnp.exp(-jnp.abs(nl)))
    a = jnp.exp(-LRU_C * r * softplus)
    mult = jnp.sqrt(1.0 - a * a)
    row = lax.broadcasted_iota(jnp.int32, a.shape, 0)
    if reset_first:
        first = jnp.logical_and(row == 0, t == 0)
        a = jnp.where(first, 0.0, a)
        mult = jnp.where(first, 1.0, mult)
    b = mult * i * xc

    d = 1
    while d < tt:
        valid = row >= d
        b = jnp.where(valid, a * pltpu.roll(b, d, axis=0) + b, b)
        a = jnp.where(valid, a * pltpu.roll(a, d, axis=0), a)
        d *= 2
    h = a * h_scr[...] + b
    h_scr[...] = h[tt - 1:tt, :]
    hg_ref[...] = (h * gy_ref[...].astype(F32)).astype(hg_ref.dtype)

    @pl.when(t == nt - 1)
    def _():
        hl_ref[...] = h[tt - 1:tt, :]


def _rglru(xb, gy, row0, n_streams, length, conv_past, h0, conv_w, conv_b, w_ra, b_ra, w_ri, b_ri, lam, reset_first):
    c = xb.shape[1]
    nb, bw, _ = w_ra.shape
    tt = _divisor_tile(length, 256, SUBLANES)
    assert tt >= SUBLANES and (tt & (tt - 1)) == 0 and row0 % tt == 0
    nt = length // tt
    blk0 = row0 // tt
    past8 = jnp.concatenate(
        [jnp.zeros((n_streams, SUBLANES - (CONV_W - 1), c), F32), conv_past.astype(F32)], axis=1)
    vec = lambda v: v.reshape(1, c).astype(F32)
    rows = lambda b, j, t: (blk0 + b * nt + t, j)
    chan = lambda b, j, t: (0, j)
    state = lambda b, j, t: (b, 0, j)
    wblk = lambda b, j, t: (j, 0, 0)
    hg, h_last = pl.pallas_call(
        functools.partial(_rglru_kernel, tt=tt, reset_first=reset_first),
        out_shape=(jax.ShapeDtypeStruct((n_streams * length, c), BF16),
                   jax.ShapeDtypeStruct((n_streams, 1, c), F32)),
        grid=(n_streams, nb, nt),
        in_specs=[pl.BlockSpec((tt, bw), rows), pl.BlockSpec((tt, bw), rows),
                  pl.BlockSpec((None, SUBLANES, bw), state), pl.BlockSpec((None, 1, bw), state),
                  pl.BlockSpec((CONV_W, bw), chan), pl.BlockSpec((1, bw), chan),
                  pl.BlockSpec((None, bw, bw), wblk), pl.BlockSpec((1, bw), chan),
                  pl.BlockSpec((None, bw, bw), wblk), pl.BlockSpec((1, bw), chan),
                  pl.BlockSpec((1, bw), chan)],
        out_specs=(pl.BlockSpec((tt, bw), lambda b, j, t: (b * nt + t, j)),
                   pl.BlockSpec((None, 1, bw), state)),
        scratch_shapes=[pltpu.VMEM((tt + SUBLANES, bw), F32), pltpu.VMEM((1, bw), F32)],
        compiler_params=_params(("parallel", "parallel", "arbitrary")),
        name="rglru",
    )(xb, gy, past8, h0.reshape(n_streams, 1, c).astype(F32), conv_w.astype(F32), vec(conv_b),
      w_ra.astype(BF16), vec(b_ra), w_ri.astype(BF16), vec(b_ri), vec(lam))
    return hg, h_last.reshape(n_streams, c)


def _router_kernel(h_ref, g_ref, wr_ref, br_ref, topi_ref, prob_ref, rank_ref, cnt_ref, carry_scr, *, n_exp, tm):
    step = pl.program_id(0)

    @pl.when(step == 0)
    def _():
        carry_scr[...] = jnp.zeros(carry_scr.shape, F32)

    hn = _rms(h_ref[...], g_ref[...])
    logits = lax.dot_general(wr_ref[...], hn, (((1,), (1,)), ((), ())),
                             precision=lax.Precision.HIGHEST, preferred_element_type=F32) + br_ref[...]
    eidx = lax.broadcasted_iota(jnp.int32, (n_exp, tm), 0)
    vals = logits
    top_v, top_i = [], []
    for _ in range(TOP_K):
        m = jnp.max(vals, axis=0, keepdims=True)
        sel = jnp.min(jnp.where(vals == m, eidx, n_exp), axis=0, keepdims=True)
        top_v.append(m)
        top_i.append(sel)
        vals = jnp.where(eidx == sel, -jnp.inf, vals)
    ex = [jnp.exp(v - top_v[0]) for v in top_v]
    denom = ex[0]
    for e in ex[1:]:
        denom = denom + e
    onehot = jnp.zeros((n_exp, tm), F32)
    for sel in top_i:
        onehot = onehot + (eidx == sel).astype(F32)
    upper = (lax.broadcasted_iota(jnp.int32, (tm, tm), 0) < lax.broadcasted_iota(jnp.int32, (tm, tm), 1))
    before = jnp.dot(onehot.astype(BF16), upper.astype(BF16), preferred_element_type=F32) + carry_scr[...]
    for k in range(TOP_K):
        topi_ref[k:k + 1, :] = top_i[k]
        prob_ref[k:k + 1, :] = ex[k] / denom
        rank_ref[k:k + 1, :] = jnp.sum(jnp.where(eidx == top_i[k], before, 0.0), axis=0,
                                       keepdims=True).astype(jnp.int32)
    carry_scr[...] = carry_scr[...] + jnp.sum(onehot, axis=1, keepdims=True)
    cnt_ref[...] = carry_scr[...].astype(jnp.int32)


def _router(h1, g_ffn, w_router, b_router):
    t, d = h1.shape
    n_exp = w_router.shape[1]
    tm = _divisor_tile(t, 512, LANES)
    kt = lambda i: (0, i)
    fixed = lambda i: (0, 0)
    return pl.pallas_call(
        functools.partial(_router_kernel, n_exp=n_exp, tm=tm),
        out_shape=(jax.ShapeDtypeStruct((TOP_K, t), jnp.int32), jax.ShapeDtypeStruct((TOP_K, t), F32),
                   jax.ShapeDtypeStruct((TOP_K, t), jnp.int32), jax.ShapeDtypeStruct((n_exp, 1), jnp.int32)),
        grid=(t // tm,),
        in_specs=[pl.BlockSpec((tm, d), lambda i: (i, 0)), pl.BlockSpec((1, d), fixed),
                  pl.BlockSpec((n_exp, d), fixed), pl.BlockSpec((n_exp, 1), fixed)],
        out_specs=(pl.BlockSpec((TOP_K, tm), kt), pl.BlockSpec((TOP_K, tm), kt), pl.BlockSpec((TOP_K, tm), kt),
                   pl.BlockSpec((n_exp, 1), fixed)),
        scratch_shapes=[pltpu.VMEM((n_exp, 1), F32)],
        compiler_params=_params(("arbitrary",)),
        name="router",
    )(h1, g_ffn.reshape(1, d), w_router.T.astype(F32), b_router.reshape(n_exp, 1).astype(F32))


GATHER_UNROLL = 8
NORM_ROW_CHUNK = 64


def _gather_norm_kernel(src_ref, nxt_ref, h_hbm, g_ref, o_ref, buf, sem, *, rows):
    i = pl.program_id(0)
    nt = pl.num_programs(0)
    slot = i % 2

    def issue(idx_ref, s):
        def start(r, c):
            pltpu.make_async_copy(h_hbm.at[pl.ds(idx_ref[0, r], 1), :], buf.at[s, pl.ds(r, 1), :],
                                  sem.at[s]).start()
            return c
        lax.fori_loop(0, rows, start, 0, unroll=GATHER_UNROLL)

    @pl.when(i == 0)
    def _():
        issue(src_ref, 0)

    @pl.when(i + 1 < nt)
    def _():
        issue(nxt_ref, 1 - slot)

    pltpu.make_async_copy(h_hbm.at[pl.ds(0, rows), :], buf.at[slot], sem.at[slot]).wait()
    for c0 in range(0, rows, NORM_ROW_CHUNK):
        c1 = min(c0 + NORM_ROW_CHUNK, rows)
        o_ref[c0:c1, :] = _rms(buf[slot, c0:c1, :], g_ref[...]).astype(o_ref.dtype)


def _gather_norm(h1, g_ffn, src_token, rows):
    t, d = h1.shape
    n_rows = src_token.shape[0]
    nt = n_rows // rows
    src_tiles = src_token.reshape(nt, 1, rows)
    return pl.pallas_call(
        functools.partial(_gather_norm_kernel, rows=rows),
        out_shape=jax.ShapeDtypeStruct((n_rows, d), BF16),
        grid=(nt,),
        in_specs=[pl.BlockSpec((None, 1, rows), lambda i: (i, 0, 0), memory_space=pltpu.SMEM),
                  pl.BlockSpec((None, 1, rows), lambda i: (jnp.minimum(i + 1, nt - 1), 0, 0),
                               memory_space=pltpu.SMEM),
                  pl.BlockSpec(memory_space=pl.ANY), pl.BlockSpec((1, d), lambda i: (0, 0))],
        out_specs=pl.BlockSpec((rows, d), lambda i: (i, 0)),
        scratch_shapes=[pltpu.VMEM((2, rows, d), F32), pltpu.SemaphoreType.DMA((2,))],
        compiler_params=_params(("arbitrary",)),
        name="gather_norm",
    )(src_tiles, src_tiles, h1, g_ffn.reshape(1, d))


def _expert_up_kernel(te_ref, nw_ref, nu_ref, x_ref, wg_ref, bg_ref, wu_ref, bu_ref, o_ref, wg_scr, wu_scr):
    m = pl.program_id(1)

    @pl.when(nw_ref[m] == 1)
    def _():
        wg_scr[...] = wg_ref[...].astype(BF16)
        wu_scr[...] = wu_ref[...].astype(BF16)

    @pl.when(m < nu_ref[0])
    def _():
        x = x_ref[...]
        g = jnp.dot(x, wg_scr[...], preferred_element_type=F32) + bg_ref[...]
        u = jnp.dot(x, wu_scr[...], preferred_element_type=F32) + bu_ref[...]
        g = jnp.minimum(g, SWIGLU_LIMIT)
        u = jnp.clip(u, -SWIGLU_LIMIT, SWIGLU_LIMIT)
        o_ref[...] = ((u + 1.0) * (g * jax.nn.sigmoid(SWIGLU_ALPHA * g))).astype(o_ref.dtype)

    @pl.when(m >= nu_ref[0])
    def _():
        o_ref[...] = jnp.zeros(o_ref.shape, o_ref.dtype)


def _expert_down_kernel(te_ref, nw_ref, nu_ref, x_ref, wd_ref, bd_ref, o_ref, wd_scr):
    m = pl.program_id(1)

    @pl.when(nw_ref[m] == 1)
    def _():
        wd_scr[...] = wd_ref[...].astype(BF16)

    @pl.when(m < nu_ref[0])
    def _():
        o_ref[...] = jnp.dot(x_ref[...], wd_scr[...], preferred_element_type=F32) + bd_ref[...]

    @pl.when(m >= nu_ref[0])
    def _():
        o_ref[...] = jnp.zeros(o_ref.shape, o_ref.dtype)


def _expert_up(xbuf, tile_expert, new_weights, n_used, w_gate, b_gate, w_up, b_up, tm, tn):
    n_rows, d = xbuf.shape
    n_exp, _, d_ff = w_gate.shape
    tn = _divisor_tile(d_ff, tn, 2 * LANES)
    wmap = lambda j, i, te, nw, nu: (te[i], 0, j)
    return pl.pallas_call(
        _expert_up_kernel,
        out_shape=jax.ShapeDtypeStruct((n_rows, d_ff), BF16),
        grid_spec=pltpu.PrefetchScalarGridSpec(
            num_scalar_prefetch=3,
            grid=(d_ff // tn, n_rows // tm),
            in_specs=[pl.BlockSpec((tm, d), lambda j, i, te, nw, nu: (i, 0)),
                      pl.BlockSpec((None, d, tn), wmap), pl.BlockSpec((None, 1, tn), wmap),
                      pl.BlockSpec((None, d, tn), wmap), pl.BlockSpec((None, 1, tn), wmap)],
            out_specs=pl.BlockSpec((tm, tn), lambda j, i, te, nw, nu: (i, j)),
            scratch_shapes=[pltpu.VMEM((d, tn), BF16), pltpu.VMEM((d, tn), BF16)]),
        compiler_params=_params(("arbitrary", "arbitrary")),
        name="expert_up",
    )(tile_expert, new_weights, n_used, xbuf, w_gate, b_gate.reshape(n_exp, 1, d_ff), w_up,
      b_up.reshape(n_exp, 1, d_ff))


def _expert_down(hbuf, tile_expert, new_weights, n_used, w_down, b_down, tm, tn):
    n_rows, d_ff = hbuf.shape
    n_exp, _, d = w_down.shape
    tn = _divisor_tile(d, tn, 2 * LANES)
    wmap = lambda j, i, te, nw, nu: (te[i], 0, j)
    return pl.pallas_call(
        _expert_down_kernel,
        out_shape=jax.ShapeDtypeStruct((n_rows, d), F32),
        grid_spec=pltpu.PrefetchScalarGridSpec(
            num_scalar_prefetch=3,
            grid=(d // tn, n_rows // tm),
            in_specs=[pl.BlockSpec((tm, d_ff), lambda j, i, te, nw, nu: (i, 0)),
                      pl.BlockSpec((None, d_ff, tn), wmap), pl.BlockSpec((None, 1, tn), wmap)],
            out_specs=pl.BlockSpec((tm, tn), lambda j, i, te, nw, nu: (i, j)),
            scratch_shapes=[pltpu.VMEM((d_ff, tn), BF16)]),
        compiler_params=_params(("arbitrary", "arbitrary")),
        name="expert_down",
    )(tile_expert, new_weights, n_used, hbuf, w_down, b_down.reshape(n_exp, 1, d))


def _combine_kernel(dest_ref, nxt_ref, h_ref, p_ref, y_hbm, g_ref, o_ref, buf, sem, *, tm, final_norm):
    i = pl.program_id(0)
    nt = pl.num_programs(0)
    slot = i % 2

    def issue(idx_ref, s):
        def start(r, c):
            for k in range(TOP_K):
                pltpu.make_async_copy(y_hbm.at[pl.ds(idx_ref[k, r], 1), :], buf.at[s, pl.ds(k * tm + r, 1), :],
                                      sem.at[s]).start()
            return c
        lax.fori_loop(0, tm, start, 0, unroll=GATHER_UNROLL // 2)

    @pl.when(i == 0)
    def _():
        issue(dest_ref, 0)

    @pl.when(i + 1 < nt)
    def _():
        issue(nxt_ref, 1 - slot)

    pltpu.make_async_copy(y_hbm.at[pl.ds(0, TOP_K * tm), :], buf.at[slot], sem.at[slot]).wait()
    acc = h_ref[...]
    for k in range(TOP_K):
        acc = acc + buf[slot, k * tm:(k + 1) * tm, :] * p_ref[:, k:k + 1]
    o_ref[...] = _rms(acc, g_ref[...]) if final_norm else acc


def _combine(h1, probs_kt, dest_kt, ybuf, g_final, final_norm, tm=128):
    t, d = h1.shape
    tm = _divisor_tile(t, tm, SUBLANES)
    nt = t // tm
    dest_tiles = dest_kt.reshape(TOP_K, nt, tm).transpose(1, 0, 2)
    return pl.pallas_call(
        functools.partial(_combine_kernel, tm=tm, final_norm=final_norm),
        out_shape=jax.ShapeDtypeStruct((t, d), F32),
        grid=(nt,),
        in_specs=[pl.BlockSpec((None, TOP_K, tm), lambda i: (i, 0, 0), memory_space=pltpu.SMEM),
                  pl.BlockSpec((None, TOP_K, tm), lambda i: (jnp.minimum(i + 1, nt - 1), 0, 0),
                               memory_space=pltpu.SMEM),
                  pl.BlockSpec((tm, d), lambda i: (i, 0)), pl.BlockSpec((tm, TOP_K), lambda i: (i, 0)),
                  pl.BlockSpec(memory_space=pl.ANY), pl.BlockSpec((1, d), lambda i: (0, 0))],
        out_specs=pl.BlockSpec((tm, d), lambda i: (i, 0)),
        scratch_shapes=[pltpu.VMEM((2, TOP_K * tm, d), F32), pltpu.SemaphoreType.DMA((2,))],
        compiler_params=_params(("arbitrary",)),
        name="combine",
    )(dest_tiles, dest_tiles, h1, probs_kt.T, ybuf, g_final.reshape(1, d))


EXPERT_ROW_TILE = 512
EXPERT_COL_TILE = 512


def _rope_tables(pos, rope_dim):
    half = rope_dim // 2
    inv = ROPE_THETA ** (-jnp.arange(half, dtype=F32) / half)
    ang = pos.astype(F32)[:, None] * inv[None, :]
    cos, sin = jnp.cos(ang), jnp.sin(ang)
    zeros = jnp.zeros((pos.shape[0], LANES - rope_dim), F32)
    return jnp.concatenate([cos, cos, zeros], axis=1), jnp.concatenate([-sin, sin, zeros], axis=1)


def _swap_halves(w):
    half = w.shape[-1] // 2
    return jnp.concatenate([w[..., half:], w[..., :half]], axis=-1)


def _moe(h1, g_ffn, w_router, b_router, w_gate, b_gate, w_up, b_up, w_down, b_down, g_out, final_norm):
    t, d = h1.shape
    n_exp = w_router.shape[1]
    tme = EXPERT_ROW_TILE
    top_i, probs, rank, counts = _router(h1, g_ffn, w_router, b_router)
    counts = counts.reshape(n_exp)
    padded = (counts + tme - 1) // tme * tme
    pend = jnp.cumsum(padded)
    pstart = pend - padded
    first_row = jnp.sum(jnp.where(top_i[..., None] == jnp.arange(n_exp, dtype=jnp.int32), pstart, 0), axis=-1)
    dest = first_row.astype(jnp.int32) + rank
    n_assign = t * TOP_K
    n_rows = (n_assign + tme - 1) // tme * tme + n_exp * tme
    tile_start = jnp.arange(n_rows // tme, dtype=jnp.int32) * tme
    tile_expert = jnp.minimum(jnp.sum(tile_start[:, None] >= pend[None, :], axis=1), n_exp - 1).astype(jnp.int32)
    new_weights = jnp.concatenate([jnp.ones((1,), jnp.int32),
                                   (tile_expert[1:] != tile_expert[:-1]).astype(jnp.int32)])
    n_used = (pend[-1] // tme).astype(jnp.int32).reshape(1)
    tok = jnp.tile(jnp.arange(t, dtype=jnp.int32), TOP_K)
    src_token = jnp.zeros((n_rows,), jnp.int32).at[dest.reshape(-1)].set(tok)
    xbuf = _gather_norm(h1, g_ffn, src_token, tme)
    hbuf = _expert_up(xbuf, tile_expert, new_weights, n_used, w_gate, b_gate, w_up, b_up, tme, EXPERT_COL_TILE)
    ybuf = _expert_down(hbuf, tile_expert, new_weights, n_used, w_down, b_down, tme, EXPERT_COL_TILE)
    return _combine(h1, probs, dest, ybuf, g_out, final_norm)


def _layer(x2d, geom, cc, ss, past_ckv, past_krope, conv_s, lru_s, lw, g_out, final_norm):
    (g_mix, w_in, g_q, w_uq, g_kv, w_uk, w_uv, w_o_attn, conv_w, conv_b, w_ra, b_ra, w_ri, b_ri,
     lam, w_o_rnn, w_out, g_ffn, w_router, b_router, w_gate, b_gate, w_up, b_up, w_down, b_down) = lw
    bp, lp, bs, ls, past = geom
    tp = bp * lp
    t, d = x2d.shape
    q_lora, n_heads, qk_dim = w_uq.shape
    kv_lora, _, nope = w_uk.shape
    v_dim = w_uv.shape[2]
    rope = qk_dim - nope
    c = conv_w.shape[1]
    assert nope == LANES and v_dim == LANES and 2 * rope == LANES and tp % ls == 0

    o1, o2, o3, o4 = q_lora + kv_lora, q_lora + kv_lora + rope, q_lora + kv_lora + rope + c, q_lora + kv_lora + rope + 2 * c
    w_kr = w_in[:, o1:o2]
    w_head = jnp.concatenate([w_in[:, :o1], w_kr, _swap_halves(w_kr)], axis=1).astype(BF16)
    w_xb = w_in[:, o2:o3].astype(BF16)
    w_yb = w_in[:, o3:o4].astype(BF16)
    w_mg = w_in[:, o4:].astype(BF16)
    w_qr = w_uq[:, :, nope:]
    w_q = jnp.concatenate([w_uq[:, :, :nope], w_qr, _swap_halves(w_qr)], axis=2).reshape(q_lora, -1).astype(BF16)
    w_ukv = jnp.concatenate([w_uk.reshape(kv_lora, -1), w_uv.reshape(kv_lora, -1)], axis=1).astype(BF16)

    xn = _rmsnorm_bf16(x2d, g_mix)
    cqn, ckv, ckv_b, kr, kr_b = _head_proj(xn, w_head, g_q, g_kv, cc, ss)
    xb = _matmul([(xn, w_xb)], _ep_identity, F32, name="proj_xb")
    gy = _matmul([(xn, w_yb)], _ep_gelu, BF16, name="proj_yb")
    gates = _matmul([(xn, w_mg)], _ep_sigmoid, BF16, name="proj_gates")

    scale = float(qk_dim) ** -0.5 * math.log2(math.e)
    q = _matmul([(cqn, w_q)], functools.partial(_ep_q_rope, scale=scale), BF16, row_extras=(cc, ss), name="q_up")
    kv_p = _matmul([(ckv_b[:tp], w_ukv)], _ep_identity, BF16, name="kv_up_prompt")
    o_p = _attention(q, 0, kv_p, kr_b[:tp], bp, n_heads, lp, lp, 0)
    lk_s = past + ls
    ckv_all = jnp.concatenate([past_ckv.astype(BF16), ckv_b[tp:].reshape(bs, ls, kv_lora)], axis=1)
    kr_past = jnp.pad(past_krope, ((0, 0), (0, 0), (0, LANES - rope))).astype(BF16)
    kr_all = jnp.concatenate([kr_past, kr_b[tp:].reshape(bs, ls, LANES)], axis=1)
    o_s = _attention_absorbed(q, tp // ls, ckv_all.reshape(bs * lk_s, kv_lora), kr_all.reshape(bs * lk_s, LANES),
                              w_ukv[:, :n_heads * nope], w_ukv[:, n_heads * nope:], bs, n_heads, ls, lk_s, past)
    o = jnp.concatenate([o_p, o_s], axis=0)

    hg_p, lru_p = _rglru(xb, gy, 0, bp, lp, jnp.zeros((bp, CONV_W - 1, c), F32), jnp.zeros((bp, c), F32),
                         conv_w, conv_b, w_ra, b_ra, w_ri, b_ri, lam, True)
    hg_s, lru_s_new = _rglru(xb, gy, tp, bs, ls, conv_s, lru_s, conv_w, conv_b, w_ra, b_ra, w_ri, b_ri, lam, past == 0)
    hg = jnp.concatenate([hg_p, hg_s], axis=0)
    xpad_p = jnp.concatenate([jnp.zeros((bp, CONV_W - 1, c), F32), xb[:tp].reshape(bp, lp, c)], axis=1)
    xpad_s = jnp.concatenate([conv_s.astype(F32), xb[tp:].reshape(bs, ls, c)], axis=1)
    conv_p_new, conv_s_new = xpad_p[:, -(CONV_W - 1):], xpad_s[:, -(CONV_W - 1):]

    mix = _matmul([(o, w_o_attn.astype(BF16)), (hg, w_o_rnn.astype(BF16))], _ep_merge, BF16,
                  tile_extras=((gates, 0), (gates, d)), tn=512, name="merge")
    h1 = _matmul([(mix, w_out.astype(BF16))], _ep_residual, F32, tile_extras=((x2d, 0),), name="out_proj")

    h2 = _moe(h1, g_ffn, w_router, b_router, w_gate, b_gate, w_up, b_up, w_down, b_down, g_out, final_norm)
    states_p = (ckv[:tp].reshape(bp, lp, kv_lora), kr[:tp, :rope].reshape(bp, lp, rope), conv_p_new, lru_p)
    states_s = (ckv[tp:].reshape(bs, ls, kv_lora), kr[tp:, :rope].reshape(bs, ls, rope), conv_s_new, lru_s_new)
    return h2, states_p, states_s


def kernel(x_prompt, x_sample, cache_ckv, cache_krope, state_conv, state_lru, g_mix, w_in, g_q, w_uq, g_kv, w_uk, w_uv, w_o_attn, conv_w, conv_b, w_ra, b_ra, w_ri, b_ri, lam, w_o_rnn, w_out, g_ffn, w_router, b_router, w_gate, b_gate, w_up, b_up, w_down, b_down, g_final):
    bp, lp, d = x_prompt.shape
    bs, ls, _ = x_sample.shape
    depth, _, past, _ = cache_ckv.shape
    rope = cache_krope.shape[-1]
    geom = (bp, lp, bs, ls, past)
    pos = jnp.concatenate([jnp.tile(jnp.arange(lp, dtype=jnp.int32), bp),
                           jnp.tile(past + jnp.arange(ls, dtype=jnp.int32), bs)])
    cc, ss = _rope_tables(pos, rope)
    h = jnp.concatenate([x_prompt.reshape(bp * lp, d), x_sample.reshape(bs * ls, d)], axis=0)
    weights = (g_mix, w_in, g_q, w_uq, g_kv, w_uk, w_uv, w_o_attn, conv_w, conv_b, w_ra, b_ra, w_ri, b_ri,
               lam, w_o_rnn, w_out, g_ffn, w_router, b_router, w_gate, b_gate, w_up, b_up, w_down, b_down)
    st_p, st_s = [], []
    for layer in range(depth):
        lw = tuple(w[layer] for w in weights)
        last = layer == depth - 1
        h, sp, ss_ = _layer(h, geom, cc, ss, cache_ckv[layer], cache_krope[layer], state_conv[layer],
                            state_lru[layer], lw, g_final if last else g_mix[layer], last)
        st_p.append(sp)
        st_s.append(ss_)
    tp = bp * lp
    outs = [h[:tp].reshape(bp, lp, d), h[tp:].reshape(bs, ls, d)]
    for group in (st_p, st_s):
        for k in range(4):
            outs.append(jnp.stack([s[k] for s in group]))
    return tuple(outs)
```

```python
import functools
import math

import jax
import jax.numpy as jnp
import numpy as np
from jax import lax
from jax.experimental import pallas as pl
from jax.experimental.pallas import tpu as pltpu

CHUNK = 64
ROPE_THETA = 10000.0
LRU_C = 8.0
TOP_K = 4
SWIGLU_LIMIT = 7.0
SWIGLU_ALPHA = 1.702
NORM_EPS = 1e-6
NEG_INF = -1e30

LANES = 128
SUBLANES = 8
V7X_VMEM_LIMIT_BYTES = 56 * 1024 * 1024

BF16 = jnp.bfloat16
F32 = jnp.float32


def _divisor_tile(n, pref, mult):
    if n <= pref:
        return n
    t = (pref // mult) * mult
    while t >= mult:
        if n % t == 0:
            return t
        t -= mult
    return n


def _params(sem):
    return pltpu.CompilerParams(dimension_semantics=sem, vmem_limit_bytes=V7X_VMEM_LIMIT_BYTES)


def _rms(x, g):
    return x * lax.rsqrt(jnp.mean(x * x, axis=-1, keepdims=True) + NORM_EPS) * g


def _row_sources(x):
    return tuple(x) if isinstance(x, (tuple, list)) else (x,)


def _row_tile(sources, pref, mult):
    return _divisor_tile(math.gcd(*[s.shape[0] for s in sources]), pref, mult)


def _row_specs(sources, tm, width, col_of, row_of):
    if len(sources) == 1:
        return [pl.BlockSpec((tm, width), lambda *g: (row_of(*g), col_of(*g)))]
    ntp = sources[0].shape[0] // tm
    return [pl.BlockSpec((tm, width), lambda *g: (jnp.minimum(row_of(*g), ntp - 1), col_of(*g))),
            pl.BlockSpec((tm, width), lambda *g: (jnp.maximum(row_of(*g) - ntp, 0), col_of(*g)))]


def _rmsnorm_kernel(*refs, ntp):
    g_ref, o_ref = refs[-2], refs[-1]
    if ntp is None:
        o_ref[...] = _rms(refs[0][...], g_ref[...]).astype(o_ref.dtype)
        return
    i = pl.program_id(0)

    @pl.when(i < ntp)
    def _():
        o_ref[...] = _rms(refs[0][...], g_ref[...]).astype(o_ref.dtype)

    @pl.when(i >= ntp)
    def _():
        o_ref[...] = _rms(refs[1][...], g_ref[...]).astype(o_ref.dtype)


def _rmsnorm_bf16(x, g):
    src = _row_sources(x)
    d = src[0].shape[1]
    t = sum(s.shape[0] for s in src)
    tm = _row_tile(src, 256, SUBLANES)
    ntp = src[0].shape[0] // tm if len(src) == 2 else None
    return pl.pallas_call(
        functools.partial(_rmsnorm_kernel, ntp=ntp),
        out_shape=jax.ShapeDtypeStruct((t, d), BF16),
        grid=(t // tm,),
        in_specs=_row_specs(src, tm, d, lambda i: 0, lambda i: i) + [pl.BlockSpec((1, d), lambda i: (0, 0))],
        out_specs=pl.BlockSpec((tm, d), lambda i: (i, 0)),
        compiler_params=_params(("parallel",)),
        name="rmsnorm",
    )(*src, g.reshape(1, d))


def _matmul_kernel(*refs, layout, ntp, epilogue, n_out):
    out_refs = refs[len(refs) - n_out:]

    def body(pick):
        groups, k = {"a": [], "b": [], "tile": [], "row": []}, 0
        for kind, cnt in layout:
            groups[kind].append(refs[k + (pick if cnt == 2 else 0)])
            k += cnt
        accs = [jnp.dot(a[...], b[...], preferred_element_type=F32) for a, b in zip(groups["a"], groups["b"])]
        res = epilogue(accs, [r[...] for r in groups["tile"]], [r[...] for r in groups["row"]])
        out_refs[0][...] = res.astype(out_refs[0].dtype)
        if n_out == 2:
            for c in range(out_refs[1].shape[1]):
                out_refs[1][:, c, :] = res[:, c * LANES:(c + 1) * LANES].astype(out_refs[1].dtype)

    if ntp is None:
        body(0)
        return
    i = pl.program_id(1)
    pl.when(i < ntp)(lambda: body(0))
    pl.when(i >= ntp)(lambda: body(1))


def _matmul(pairs, epilogue, out_dtype, tile_extras=(), row_extras=(), tm=512, tn=1024, row_major_copy=False,
            name="matmul"):
    all_rows = [_row_sources(a) for a, _ in pairs] + [_row_sources(e) for e, _ in tile_extras]
    all_rows += [_row_sources(e) for e in row_extras]
    m = sum(s.shape[0] for s in all_rows[0])
    n = pairs[0][1].shape[1]
    tm = _row_tile([s for src in all_rows for s in src], tm, 16)
    tn = _divisor_tile(n, tn, 2 * LANES)
    two = [src for src in all_rows if len(src) == 2]
    ntp = two[0][0].shape[0] // tm if two else None
    assert all(src[0].shape[0] == two[0][0].shape[0] for src in two)
    row_of = lambda j, i: i
    in_specs, args, layout = [], [], []
    for a, b in pairs:
        src = _row_sources(a)
        in_specs += _row_specs(src, tm, src[0].shape[1], lambda j, i: 0, row_of)
        in_specs.append(pl.BlockSpec((b.shape[0], tn), lambda j, i: (0, j)))
        args += [*src, b]
        layout += [("a", len(src)), ("b", 1)]
    for e, col0 in tile_extras:
        assert col0 % tn == 0
        src = _row_sources(e)
        in_specs += _row_specs(src, tm, tn, functools.partial(lambda j, i, jb: j + jb, jb=col0 // tn), row_of)
        args += [*src]
        layout.append(("tile", len(src)))
    for e in row_extras:
        src = _row_sources(e)
        in_specs += _row_specs(src, tm, src[0].shape[1], lambda j, i: 0, row_of)
        args += [*src]
        layout.append(("row", len(src)))
    out_shape = [jax.ShapeDtypeStruct((m, n), out_dtype)]
    out_specs = [pl.BlockSpec((tm, tn), lambda j, i: (i, j))]
    if row_major_copy:
        assert (tn // LANES) % SUBLANES == 0 or tn == n
        out_shape.append(jax.ShapeDtypeStruct((m, n // LANES, LANES), out_dtype))
        out_specs.append(pl.BlockSpec((tm, tn // LANES, LANES), lambda j, i: (i, j, 0)))
    kern = functools.partial(_matmul_kernel, layout=tuple(layout), ntp=ntp, epilogue=epilogue, n_out=len(out_shape))
    res = pl.pallas_call(
        kern,
        out_shape=tuple(out_shape),
        grid=(n // tn, m // tm),
        in_specs=in_specs,
        out_specs=tuple(out_specs),
        compiler_params=_params(("parallel", "parallel")),
        name=name,
    )(*args)
    return res if row_major_copy else res[0]


def _ep_identity(accs, tiles, rows):
    return accs[0]


def _ep_gelu(accs, tiles, rows):
    return jax.nn.gelu(accs[0])


def _ep_sigmoid(accs, tiles, rows):
    return jax.nn.sigmoid(accs[0])


def _rope_block(blk, cc, ss):
    return blk * cc + pltpu.roll(blk, LANES // 2, axis=1) * ss


def _ep_q_rope(accs, tiles, rows, *, scale):
    acc = accs[0]
    cc, ss = rows
    out = []
    for h in range(acc.shape[1] // (2 * LANES)):
        base = h * 2 * LANES
        out.append(acc[:, base:base + LANES] * scale)
        out.append(_rope_block(acc[:, base + LANES:base + 2 * LANES], cc, ss) * scale)
    return jnp.concatenate(out, axis=1)


def _ep_merge(accs, tiles, rows):
    return tiles[0].astype(F32) * accs[0] + tiles[1].astype(F32) * accs[1]


def _ep_residual(accs, tiles, rows):
    return tiles[0] + accs[0]


def _head_kernel(x_ref, w_ref, gq_ref, gkv_ref, cc_ref, ss_ref,
                 cqn_ref, ckv_ref, ckvb_ref, kr_ref, krb_ref, *, q_lora, kv_lora):
    p = jnp.dot(x_ref[...], w_ref[...], preferred_element_type=F32)
    cqn_ref[...] = _rms(p[:, :q_lora], gq_ref[...]).astype(cqn_ref.dtype)
    ckv = _rms(p[:, q_lora:q_lora + kv_lora], gkv_ref[...])
    ckv_ref[...] = ckv
    ckvb_ref[...] = ckv.astype(ckvb_ref.dtype)
    kr = _rope_block(p[:, q_lora + kv_lora:], cc_ref[...], ss_ref[...])
    kr_ref[...] = kr
    krb_ref[...] = kr.astype(krb_ref.dtype)


def _head_proj(xn, w_head, g_q, g_kv, cc, ss):
    t, d = xn.shape
    q_lora, kv_lora = g_q.shape[0], g_kv.shape[0]
    nh = w_head.shape[1]
    tm = _divisor_tile(t, 512, 16)
    row = lambda i: (i, 0)
    fixed = lambda i: (0, 0)
    return pl.pallas_call(
        functools.partial(_head_kernel, q_lora=q_lora, kv_lora=kv_lora),
        out_shape=(jax.ShapeDtypeStruct((t, q_lora), BF16),
                   jax.ShapeDtypeStruct((t, kv_lora), F32),
                   jax.ShapeDtypeStruct((t, kv_lora), BF16),
                   jax.ShapeDtypeStruct((t, LANES), F32),
                   jax.ShapeDtypeStruct((t, LANES), BF16)),
        grid=(t // tm,),
        in_specs=[pl.BlockSpec((tm, d), row), pl.BlockSpec((d, nh), fixed),
                  pl.BlockSpec((1, q_lora), fixed), pl.BlockSpec((1, kv_lora), fixed),
                  pl.BlockSpec((tm, LANES), row), pl.BlockSpec((tm, LANES), row)],
        out_specs=(pl.BlockSpec((tm, q_lora), row), pl.BlockSpec((tm, kv_lora), row),
                   pl.BlockSpec((tm, kv_lora), row), pl.BlockSpec((tm, LANES), row),
                   pl.BlockSpec((tm, LANES), row)),
        compiler_params=_params(("parallel",)),
        name="head_proj",
    )(xn, w_head, g_q.reshape(1, -1), g_kv.reshape(1, -1), cc, ss)


def _key_block_range(q_pos_first, q_pos_last, tk, nk_total):
    n_full = jnp.minimum(((q_pos_first // CHUNK + 1) * CHUNK) // tk, nk_total)
    n_need = jnp.minimum(((q_pos_last // CHUNK + 1) * CHUNK + tk - 1) // tk, nk_total)
    return n_full, n_need


def _softmax_step(s, v, m_ref, l_ref, acc_ref):
    n_s = s.shape[1] // LANES
    s_tiles = [s[:, c * LANES:(c + 1) * LANES] for c in range(n_s)]
    if s.shape[1] % LANES:
        tail = s[:, n_s * LANES:]
        s_tiles.append(jnp.concatenate([tail, jnp.full((s.shape[0], LANES - tail.shape[1]), NEG_INF, F32)], axis=1))
    tile_max = functools.reduce(jnp.maximum, s_tiles)
    m_prev = m_ref[...]
    m_new = jnp.maximum(m_prev, jnp.max(tile_max, axis=1, keepdims=True))
    alpha = jnp.exp2(m_prev - m_new)
    p_tiles = [jnp.exp2(t - m_new) for t in s_tiles]
    tile_sum = functools.reduce(jnp.add, p_tiles)
    l_ref[...] = alpha * l_ref[...] + jnp.sum(tile_sum, axis=1, keepdims=True)
    p = jnp.concatenate(p_tiles, axis=1)[:, :s.shape[1]].astype(BF16)
    pv = jnp.dot(p, v, preferred_element_type=F32)
    for c in range(acc_ref.shape[1] // LANES):
        cols = slice(c * LANES, (c + 1) * LANES)
        acc_ref[:, cols] = alpha * acc_ref[:, cols] + pv[:, cols]
    m_ref[...] = m_new


def _attn_kernel(q_ref, k_ref, kr_ref, v_ref, o_ref, m_scr, l_scr, acc_scr, *, hp, lq, lk, tq, tk, q_pos0):
    nq = lq // tq
    nk_total = lk // tk

    def q_block(qi, carry):
        r0 = pl.multiple_of(qi * tq, tq)
        m_scr[...] = jnp.full(m_scr.shape, NEG_INF, F32)
        l_scr[...] = jnp.zeros(l_scr.shape, F32)
        acc_scr[...] = jnp.zeros(acc_scr.shape, F32)
        n_full, n_need = _key_block_range(q_pos0 + r0, q_pos0 + r0 + tq - 1, tk, nk_total)

        def k_block(kj, masked):
            c0 = pl.multiple_of(kj * tk, tk)
            kr = kr_ref[pl.ds(c0, tk), :]
            if masked:
                qc = (q_pos0 + r0 + lax.broadcasted_iota(jnp.int32, (tq, tk), 0)) // CHUNK
                kc = (c0 + lax.broadcasted_iota(jnp.int32, (tq, tk), 1)) // CHUNK
                allowed = kc <= qc
            for h in range(hp):
                q = q_ref[pl.ds(r0, tq), h * 2 * LANES:(h + 1) * 2 * LANES]
                kcat = jnp.concatenate([k_ref[pl.ds(c0, tk), h * LANES:(h + 1) * LANES], kr], axis=1)
                s = lax.dot_general(q, kcat, (((1,), (1,)), ((), ())), preferred_element_type=F32)
                if masked:
                    s = jnp.where(allowed, s, NEG_INF)
                _softmax_step(s, v_ref[pl.ds(c0, tk), h * LANES:(h + 1) * LANES],
                              m_scr.at[h], l_scr.at[h], acc_scr.at[h])

        lax.fori_loop(0, n_full, lambda kj, c: (k_block(kj, False), c)[1], 0)
        lax.fori_loop(n_full, n_need, lambda kj, c: (k_block(kj, True), c)[1], 0)
        for h in range(hp):
            o_ref[pl.ds(r0, tq), h * LANES:(h + 1) * LANES] = (acc_scr[h] / l_scr[h]).astype(o_ref.dtype)
        return carry

    lax.fori_loop(0, nq, q_block, 0)


def _attention(q, q_blk0, kv, kr, n_streams, n_heads, lq, lk, q_pos0, hp=2):
    tq = _divisor_tile(lq, 512, 16)
    tk = _divisor_tile(lk, 512, 16)
    hp = hp if n_heads % hp == 0 else 1
    ng = n_heads // hp
    kern = functools.partial(_attn_kernel, hp=hp, lq=lq, lk=lk, tq=tq, tk=tk, q_pos0=q_pos0)
    return pl.pallas_call(
        kern,
        out_shape=jax.ShapeDtypeStruct((n_streams * lq, n_heads * LANES), BF16),
        grid=(n_streams, ng),
        in_specs=[pl.BlockSpec((lq, hp * 2 * LANES), lambda b, g: (q_blk0 + b, g)),
                  pl.BlockSpec((lk, hp * LANES), lambda b, g: (b, g)),
                  pl.BlockSpec((lk, LANES), lambda b, g: (b, 0)),
                  pl.BlockSpec((lk, hp * LANES), lambda b, g: (b, ng + g))],
        out_specs=pl.BlockSpec((lq, hp * LANES), lambda b, g: (b, g)),
        scratch_shapes=[pltpu.VMEM((hp, tq, LANES), F32), pltpu.VMEM((hp, tq, LANES), F32),
                        pltpu.VMEM((hp, tq, LANES), F32)],
        compiler_params=_params(("parallel", "parallel")),
        name="attention",
    )(q, kv, kr, kv)


def _attn_absorbed_kernel(q_ref, ckv_ref, kr_ref, wuk_ref, wuv_ref, o_ref, qcat_scr, m_scr, l_scr, acc_scr,
                          *, n_heads, lq, lk, tk, q_pos0):
    kvl = ckv_ref.shape[1]
    nk_total = lk // tk
    rows = n_heads * lq
    for h in range(n_heads):
        qn = q_ref[:, h * 2 * LANES:h * 2 * LANES + LANES]
        qa = lax.dot_general(qn, wuk_ref[:, h * LANES:(h + 1) * LANES], (((1,), (1,)), ((), ())),
                             preferred_element_type=F32)
        qcat_scr[h * lq:(h + 1) * lq, 0:kvl] = qa.astype(BF16)
        qcat_scr[h * lq:(h + 1) * lq, kvl:] = q_ref[:, h * 2 * LANES + LANES:(h + 1) * 2 * LANES]
    m_scr[...] = jnp.full(m_scr.shape, NEG_INF, F32)
    l_scr[...] = jnp.zeros(l_scr.shape, F32)
    acc_scr[...] = jnp.zeros(acc_scr.shape, F32)
    n_full, n_need = _key_block_range(q_pos0, q_pos0 + lq - 1, tk, nk_total)

    def k_block(kj, masked):
        c0 = pl.multiple_of(kj * tk, tk)
        ck = ckv_ref[pl.ds(c0, tk), :]
        kcat = jnp.concatenate([ck, kr_ref[pl.ds(c0, tk), :]], axis=1)
        s = lax.dot_general(qcat_scr[...], kcat, (((1,), (1,)), ((), ())), preferred_element_type=F32)
        if masked:
            qc = (q_pos0 + lax.broadcasted_iota(jnp.int32, (rows, tk), 0) % lq) // CHUNK
            kc = (c0 + lax.broadcasted_iota(jnp.int32, (rows, tk), 1)) // CHUNK
            s = jnp.where(kc <= qc, s, NEG_INF)
        _softmax_step(s, ck, m_scr, l_scr, acc_scr)

    lax.fori_loop(0, n_full, lambda kj, c: (k_block(kj, False), c)[1], 0)
    lax.fori_loop(n_full, n_need, lambda kj, c: (k_block(kj, True), c)[1], 0)
    for h in range(n_heads):
        inv_l = 1.0 / l_scr[h * lq:(h + 1) * lq, :]
        oh = jnp.concatenate([acc_scr[h * lq:(h + 1) * lq, c * LANES:(c + 1) * LANES] * inv_l
                              for c in range(kvl // LANES)], axis=1)
        o_ref[:, h * LANES:(h + 1) * LANES] = jnp.dot(
            oh.astype(BF16), wuv_ref[:, h * LANES:(h + 1) * LANES], preferred_element_type=F32).astype(o_ref.dtype)


def _attention_absorbed(q, q_blk0, ckv_all, kr_all, w_uk, w_uv, n_streams, n_heads, lq, lk, q_pos0):
    kvl = ckv_all.shape[1]
    tk = _divisor_tile(lk, 512, 16)
    rows = n_heads * lq
    kern = functools.partial(_attn_absorbed_kernel, n_heads=n_heads, lq=lq, lk=lk, tk=tk, q_pos0=q_pos0)
    fixed = lambda b: (0, 0)
    return pl.pallas_call(
        kern,
        out_shape=jax.ShapeDtypeStruct((n_streams * lq, n_heads * LANES), BF16),
        grid=(n_streams,),
        in_specs=[pl.BlockSpec((lq, n_heads * 2 * LANES), lambda b: (q_blk0 + b, 0)),
                  pl.BlockSpec((lk, kvl), lambda b: (b, 0)), pl.BlockSpec((lk, LANES), lambda b: (b, 0)),
                  pl.BlockSpec((kvl, n_heads * LANES), fixed), pl.BlockSpec((kvl, n_heads * LANES), fixed)],
        out_specs=pl.BlockSpec((lq, n_heads * LANES), lambda b: (b, 0)),
        scratch_shapes=[pltpu.VMEM((rows, kvl + LANES), BF16), pltpu.VMEM((rows, LANES), F32),
                        pltpu.VMEM((rows, LANES), F32), pltpu.VMEM((rows, kvl), F32)],
        compiler_params=_params(("parallel",)),
        name="attention_absorbed",
    )(q, ckv_all, kr_all, w_uk, w_uv)


CONV_W = 4


def _rglru_kernel(x_ref, gy_ref, past_ref, h0_ref, cw_ref, cb_ref, wra_ref, bra_ref, wri_ref, bri_ref, lam_ref,
                  hg_ref, hl_ref, xs_scr, h_scr, *, tt, reset_first):
    t = pl.program_id(2)
    nt = pl.num_programs(2)

    @pl.when(t == 0)
    def _():
        xs_scr[0:SUBLANES, :] = past_ref[...]
        h_scr[...] = h0_ref[...]

    x = x_ref[...]
    xs_scr[SUBLANES:, :] = x
    xc = cb_ref[...]
    for j in range(CONV_W):
        start = SUBLANES - (CONV_W - 1) + j
        xc = xc + xs_scr[start:start + tt, :] * cw_ref[j:j + 1, :]
    xs_scr[0:SUBLANES, :] = x[tt - SUBLANES:, :]

    xcb = xc.astype(BF16)
    r = jax.nn.sigmoid(jnp.dot(xcb, wra_ref[...], preferred_element_type=F32) + bra_ref[...])
    i = jax.nn.sigmoid(jnp.dot(xcb, wri_ref[...], preferred_element_type=F32) + bri_ref[...])
    nl = -lam_ref[...]
    softplus = jnp.maximum(nl, 0.0) + jnp.log1p(jnp.exp(-jnp.abs(nl)))
    a = jnp.exp(-LRU_C * r * softplus)
    mult = jnp.sqrt(1.0 - a * a)
    row = lax.broadcasted_iota(jnp.int32, a.shape, 0)
    if reset_first:
        first = jnp.logical_and(row == 0, t == 0)
        a = jnp.where(first, 0.0, a)
        mult = jnp.where(first, 1.0, mult)
    b = mult * i * xc

    groups = tt // SUBLANES
    a3 = a.reshape(groups, SUBLANES, a.shape[1])
    b3 = b.reshape(groups, SUBLANES, b.shape[1])
    sub = lax.broadcasted_iota(jnp.int32, a3.shape, 1)
    d = 1
    while d < SUBLANES:
        valid = sub >= d
        b3 = jnp.where(valid, a3 * pltpu.roll(b3, d, axis=1) + b3, b3)
        a3 = jnp.where(valid, a3 * pltpu.roll(a3, d, axis=1), a3)
        d *= 2
    state = h_scr[...]
    h_groups = []
    for k in range(groups):
        hk = a3[k] * state + b3[k]
        h_groups.append(hk)
        state = hk[SUBLANES - 1:, :]
    h = jnp.concatenate(h_groups, axis=0)
    h_scr[...] = state
    hg_ref[...] = (h * gy_ref[...].astype(F32)).astype(hg_ref.dtype)

    @pl.when(t == nt - 1)
    def _():
        hl_ref[...] = h[tt - 1:tt, :]


def _rglru(xb, gy, row0, n_streams, length, conv_past, h0, conv_w, conv_b, w_ra, b_ra, w_ri, b_ri, lam, reset_first):
    c = xb.shape[1]
    nb, bw, _ = w_ra.shape
    tt = _divisor_tile(length, 256, SUBLANES)
    assert tt >= SUBLANES and (tt & (tt - 1)) == 0 and row0 % tt == 0
    nt = length // tt
    blk0 = row0 // tt
    past8 = jnp.concatenate(
        [jnp.zeros((n_streams, SUBLANES - (CONV_W - 1), c), F32), conv_past.astype(F32)], axis=1)
    vec = lambda v: v.reshape(1, c).astype(F32)
    rows = lambda b, j, t: (blk0 + b * nt + t, j)
    chan = lambda b, j, t: (0, j)
    state = lambda b, j, t: (b, 0, j)
    wblk = lambda b, j, t: (j, 0, 0)
    hg, h_last = pl.pallas_call(
        functools.partial(_rglru_kernel, tt=tt, reset_first=reset_first),
        out_shape=(jax.ShapeDtypeStruct((n_streams * length, c), BF16),
                   jax.ShapeDtypeStruct((n_streams, 1, c), F32)),
        grid=(n_streams, nb, nt),
        in_specs=[pl.BlockSpec((tt, bw), rows), pl.BlockSpec((tt, bw), rows),
                  pl.BlockSpec((None, SUBLANES, bw), state), pl.BlockSpec((None, 1, bw), state),
                  pl.BlockSpec((CONV_W, bw), chan), pl.BlockSpec((1, bw), chan),
                  pl.BlockSpec((None, bw, bw), wblk), pl.BlockSpec((1, bw), chan),
                  pl.BlockSpec((None, bw, bw), wblk), pl.BlockSpec((1, bw), chan),
                  pl.BlockSpec((1, bw), chan)],
        out_specs=(pl.BlockSpec((tt, bw), lambda b, j, t: (b * nt + t, j)),
                   pl.BlockSpec((None, 1, bw), state)),
        scratch_shapes=[pltpu.VMEM((tt + SUBLANES, bw), F32), pltpu.VMEM((1, bw), F32)],
        compiler_params=_params(("parallel", "parallel", "arbitrary")),
        name="rglru",
    )(xb, gy, past8, h0.reshape(n_streams, 1, c).astype(F32), conv_w.astype(F32), vec(conv_b),
      w_ra.astype(BF16), vec(b_ra), w_ri.astype(BF16), vec(b_ri), vec(lam))
    return hg, h_last.reshape(n_streams, c)


def _router_kernel(h_ref, g_ref, wr_ref, br_ref, topi_ref, prob_ref, rank_ref, cnt_ref, carry_scr, *, n_exp, tm):
    step = pl.program_id(0)

    @pl.when(step == 0)
    def _():
        carry_scr[...] = jnp.zeros(carry_scr.shape, F32)

    hn = _rms(h_ref[...], g_ref[...])
    logits = lax.dot_general(wr_ref[...], hn, (((1,), (1,)), ((), ())),
                             precision=lax.Precision.HIGHEST, preferred_element_type=F32) + br_ref[...]
    eidx = lax.broadcasted_iota(jnp.int32, (n_exp, tm), 0)
    vals = logits
    top_v, top_i = [], []
    for _ in range(TOP_K):
        m = jnp.max(vals, axis=0, keepdims=True)
        sel = jnp.min(jnp.where(vals == m, eidx, n_exp), axis=0, keepdims=True)
        top_v.append(m)
        top_i.append(sel)
        vals = jnp.where(eidx == sel, -jnp.inf, vals)
    ex = [jnp.exp(v - top_v[0]) for v in top_v]
    denom = ex[0]
    for e in ex[1:]:
        denom = denom + e
    onehot = jnp.zeros((n_exp, tm), F32)
    for sel in top_i:
        onehot = onehot + (eidx == sel).astype(F32)
    upper = (lax.broadcasted_iota(jnp.int32, (tm, tm), 0) < lax.broadcasted_iota(jnp.int32, (tm, tm), 1))
    before = jnp.dot(onehot.astype(BF16), upper.astype(BF16), preferred_element_type=F32) + carry_scr[...]
    for k in range(TOP_K):
        topi_ref[k:k + 1, :] = top_i[k]
        prob_ref[k:k + 1, :] = ex[k] / denom
        rank_ref[k:k + 1, :] = jnp.sum(jnp.where(eidx == top_i[k], before, 0.0), axis=0,
                                       keepdims=True).astype(jnp.int32)
    carry_scr[...] = carry_scr[...] + jnp.sum(onehot, axis=1, keepdims=True)
    cnt_ref[...] = carry_scr[...].astype(jnp.int32)


def _router(h1, g_ffn, w_router, b_router):
    t, d = h1.shape
    n_exp = w_router.shape[1]
    tm = _divisor_tile(t, 512, LANES)
    kt = lambda i: (0, i)
    fixed = lambda i: (0, 0)
    return pl.pallas_call(
        functools.partial(_router_kernel, n_exp=n_exp, tm=tm),
        out_shape=(jax.ShapeDtypeStruct((TOP_K, t), jnp.int32), jax.ShapeDtypeStruct((TOP_K, t), F32),
                   jax.ShapeDtypeStruct((TOP_K, t), jnp.int32), jax.ShapeDtypeStruct((n_exp, 1), jnp.int32)),
        grid=(t // tm,),
        in_specs=[pl.BlockSpec((tm, d), lambda i: (i, 0)), pl.BlockSpec((1, d), fixed),
                  pl.BlockSpec((n_exp, d), fixed), pl.BlockSpec((n_exp, 1), fixed)],
        out_specs=(pl.BlockSpec((TOP_K, tm), kt), pl.BlockSpec((TOP_K, tm), kt), pl.BlockSpec((TOP_K, tm), kt),
                   pl.BlockSpec((n_exp, 1), fixed)),
        scratch_shapes=[pltpu.VMEM((n_exp, 1), F32)],
        compiler_params=_params(("arbitrary",)),
        name="router",
    )(h1, g_ffn.reshape(1, d), w_router.T.astype(F32), b_router.reshape(n_exp, 1).astype(F32))


GATHER_UNROLL = 8
NORM_ROW_CHUNK = 16


def _gather_norm_kernel(src_ref, nxt_ref, h_hbm, g_ref, o_ref, buf, sem, *, rows, n_piece):
    i = pl.program_id(0)
    nt = pl.num_programs(0)
    slot = i % 2

    pitch = n_piece + SUBLANES

    def row_copy(idx_ref, s, r):
        src0 = pl.multiple_of(idx_ref[0, r] * n_piece, n_piece)
        dst0 = pl.multiple_of(r * pitch, SUBLANES)
        return pltpu.make_async_copy(h_hbm.at[pl.ds(src0, n_piece), :], buf.at[s, pl.ds(dst0, n_piece), :],
                                     sem.at[s])

    def issue(idx_ref, s):
        lax.fori_loop(0, rows, lambda r, c: (row_copy(idx_ref, s, r).start(), c)[1], 0, unroll=GATHER_UNROLL)

    @pl.when(i == 0)
    def _():
        issue(src_ref, 0)

    @pl.when(i + 1 < nt)
    def _():
        issue(nxt_ref, 1 - slot)

    lax.fori_loop(0, rows, lambda r, c: (row_copy(src_ref, slot, r).wait(), c)[1], 0, unroll=GATHER_UNROLL)
    for c0 in range(0, rows, NORM_ROW_CHUNK):
        nr = min(NORM_ROW_CHUNK, rows - c0)
        x = jnp.concatenate([buf[slot, pl.ds(c0 * pitch + p, nr, stride=pitch), :] for p in range(n_piece)], axis=1)
        o_ref[c0:c0 + nr, :] = _rms(x, g_ref[...]).astype(o_ref.dtype)


def _gather_norm(h1_rows, g_ffn, src_token, rows):
    t, n_piece, _ = h1_rows.shape
    d = n_piece * LANES
    h1_rows = h1_rows.reshape(t * n_piece, LANES)
    n_rows = src_token.shape[0]
    nt = n_rows // rows
    src_tiles = src_token.reshape(nt, 1, rows)
    return pl.pallas_call(
        functools.partial(_gather_norm_kernel, rows=rows, n_piece=n_piece),
        out_shape=jax.ShapeDtypeStruct((n_rows, d), BF16),
        grid=(nt,),
        in_specs=[pl.BlockSpec((None, 1, rows), lambda i: (i, 0, 0), memory_space=pltpu.SMEM),
                  pl.BlockSpec((None, 1, rows), lambda i: (jnp.minimum(i + 1, nt - 1), 0, 0),
                               memory_space=pltpu.SMEM),
                  pl.BlockSpec(memory_space=pl.ANY), pl.BlockSpec((1, d), lambda i: (0, 0))],
        out_specs=pl.BlockSpec((rows, d), lambda i: (i, 0)),
        scratch_shapes=[pltpu.VMEM((2, rows * (n_piece + SUBLANES), LANES), F32), pltpu.SemaphoreType.DMA((2,))],
        compiler_params=_params(("arbitrary",)),
        name="gather_norm",
    )(src_tiles, src_tiles, h1_rows, g_ffn.reshape(1, d))


def _expert_up_kernel(te_ref, nu_ref, x_ref, wg_ref, bg_ref, wu_ref, bu_ref, o_ref):
    m = pl.program_id(1)

    @pl.when(m < nu_ref[0])
    def _():
        x = x_ref[...]
        g = jnp.dot(x, wg_ref[...].astype(BF16), preferred_element_type=F32) + bg_ref[...]
        u = jnp.dot(x, wu_ref[...].astype(BF16), preferred_element_type=F32) + bu_ref[...]
        g = jnp.minimum(g, SWIGLU_LIMIT)
        u = jnp.clip(u, -SWIGLU_LIMIT, SWIGLU_LIMIT)
        o_ref[...] = ((u + 1.0) * (g * jax.nn.sigmoid(SWIGLU_ALPHA * g))).astype(o_ref.dtype)

    @pl.when(m >= nu_ref[0])
    def _():
        o_ref[...] = jnp.zeros(o_ref.shape, o_ref.dtype)


def _expert_down_kernel(te_ref, nu_ref, x_ref, wd_ref, bd_ref, o_ref):
    m = pl.program_id(1)

    @pl.when(m < nu_ref[0])
    def _():
        o_ref[...] = jnp.dot(x_ref[...], wd_ref[...].astype(BF16), preferred_element_type=F32) + bd_ref[...]

    @pl.when(m >= nu_ref[0])
    def _():
        o_ref[...] = jnp.zeros(o_ref.shape, o_ref.dtype)


def _expert_up(xbuf, tile_expert, n_used, w_gate, b_gate, w_up, b_up, tm, tn):
    n_rows, d = xbuf.shape
    n_exp, _, d_ff = w_gate.shape
    tn = _divisor_tile(d_ff, tn, 2 * LANES)
    wmap = lambda j, i, te, nu: (te[i], 0, j)
    return pl.pallas_call(
        _expert_up_kernel,
        out_shape=jax.ShapeDtypeStruct((n_rows, d_ff), BF16),
        grid_spec=pltpu.PrefetchScalarGridSpec(
            num_scalar_prefetch=2,
            grid=(d_ff // tn, n_rows // tm),
            in_specs=[pl.BlockSpec((tm, d), lambda j, i, te, nu: (i, 0)),
                      pl.BlockSpec((None, d, tn), wmap), pl.BlockSpec((None, 1, tn), wmap),
                      pl.BlockSpec((None, d, tn), wmap), pl.BlockSpec((None, 1, tn), wmap)],
            out_specs=pl.BlockSpec((tm, tn), lambda j, i, te, nu: (i, j))),
        compiler_params=_params(("parallel", "arbitrary")),
        name="expert_up",
    )(tile_expert, n_used, xbuf, w_gate, b_gate.reshape(n_exp, 1, d_ff), w_up, b_up.reshape(n_exp, 1, d_ff))


def _expert_down(hbuf, tile_expert, n_used, w_down, b_down, tm, tn):
    n_rows, d_ff = hbuf.shape
    n_exp, _, d = w_down.shape
    tn = _divisor_tile(d, tn, 2 * LANES)
    wmap = lambda j, i, te, nu: (te[i], 0, j)
    return pl.pallas_call(
        _expert_down_kernel,
        out_shape=jax.ShapeDtypeStruct((n_rows, d), F32),
        grid_spec=pltpu.PrefetchScalarGridSpec(
            num_scalar_prefetch=2,
            grid=(d // tn, n_rows // tm),
            in_specs=[pl.BlockSpec((tm, d_ff), lambda j, i, te, nu: (i, 0)),
                      pl.BlockSpec((None, d_ff, tn), wmap), pl.BlockSpec((None, 1, tn), wmap)],
            out_specs=pl.BlockSpec((tm, tn), lambda j, i, te, nu: (i, j))),
        compiler_params=_params(("parallel", "arbitrary")),
        name="expert_down",
    )(tile_expert, n_used, hbuf, w_down, b_down.reshape(n_exp, 1, d))


def _combine_kernel(dest_ref, nxt_ref, h_ref, p_ref, y_hbm, g_ref, o_ref, buf, sem, *, tm, final_norm):
    i = pl.program_id(0)
    nt = pl.num_programs(0)
    slot = i % 2

    def issue(idx_ref, s):
        def start(r, c):
            for k in range(TOP_K):
                pltpu.make_async_copy(y_hbm.at[pl.ds(idx_ref[k, r], 1), :], buf.at[s, pl.ds(k * tm + r, 1), :],
                                      sem.at[s]).start()
            return c
        lax.fori_loop(0, tm, start, 0, unroll=GATHER_UNROLL // 2)

    @pl.when(i == 0)
    def _():
        issue(dest_ref, 0)

    @pl.when(i + 1 < nt)
    def _():
        issue(nxt_ref, 1 - slot)

    pltpu.make_async_copy(y_hbm.at[pl.ds(0, TOP_K * tm), :], buf.at[slot], sem.at[slot]).wait()
    acc = h_ref[...]
    for k in range(TOP_K):
        acc = acc + buf[slot, k * tm:(k + 1) * tm, :] * p_ref[:, k:k + 1]
    o_ref[...] = _rms(acc, g_ref[...]) if final_norm else acc


def _combine(h1, probs_kt, dest_kt, ybuf, g_final, final_norm, row0, n_tok, tm=128):
    t, d = h1.shape
    tm = _divisor_tile(math.gcd(row0, n_tok) if row0 else n_tok, tm, SUBLANES)
    nt, blk0 = n_tok // tm, row0 // tm
    dest_tiles = dest_kt[:, row0:row0 + n_tok].reshape(TOP_K, nt, tm).transpose(1, 0, 2)
    return pl.pallas_call(
        functools.partial(_combine_kernel, tm=tm, final_norm=final_norm),
        out_shape=jax.ShapeDtypeStruct((n_tok, d), F32),
        grid=(nt,),
        in_specs=[pl.BlockSpec((None, TOP_K, tm), lambda i: (i, 0, 0), memory_space=pltpu.SMEM),
                  pl.BlockSpec((None, TOP_K, tm), lambda i: (jnp.minimum(i + 1, nt - 1), 0, 0),
                               memory_space=pltpu.SMEM),
                  pl.BlockSpec((tm, d), lambda i: (blk0 + i, 0)), pl.BlockSpec((tm, TOP_K), lambda i: (i, 0)),
                  pl.BlockSpec(memory_space=pl.ANY), pl.BlockSpec((1, d), lambda i: (0, 0))],
        out_specs=pl.BlockSpec((tm, d), lambda i: (i, 0)),
        scratch_shapes=[pltpu.VMEM((2, TOP_K * tm, d), F32), pltpu.SemaphoreType.DMA((2,))],
        compiler_params=_params(("arbitrary",)),
        name="combine",
    )(dest_tiles, dest_tiles, h1, probs_kt[:, row0:row0 + n_tok].T, ybuf, g_final.reshape(1, d))


EXPERT_ROW_TILE = 512
EXPERT_COL_TILE = 512


def _rope_tables(pos, rope_dim):
    half = rope_dim // 2
    inv = ROPE_THETA ** (-jnp.arange(half, dtype=F32) / half)
    ang = pos.astype(F32)[:, None] * inv[None, :]
    cos, sin = jnp.cos(ang), jnp.sin(ang)
    zeros = jnp.zeros((pos.shape[0], LANES - rope_dim), F32)
    return jnp.concatenate([cos, cos, zeros], axis=1), jnp.concatenate([-sin, sin, zeros], axis=1)


def _swap_halves(w):
    half = w.shape[-1] // 2
    return jnp.concatenate([w[..., half:], w[..., :half]], axis=-1)


def _moe(h1, h1_rows, tp, g_ffn, w_router, b_router, w_gate, b_gate, w_up, b_up, w_down, b_down, g_out,
         final_norm):
    t, d = h1.shape
    n_exp = w_router.shape[1]
    tme = EXPERT_ROW_TILE
    top_i, probs, rank, counts = _router(h1, g_ffn, w_router, b_router)
    counts = counts.reshape(n_exp)
    padded = (counts + tme - 1) // tme * tme
    pend = jnp.cumsum(padded)
    pstart = pend - padded
    first_row = jnp.sum(jnp.where(top_i[..., None] == jnp.arange(n_exp, dtype=jnp.int32), pstart, 0), axis=-1)
    dest = first_row.astype(jnp.int32) + rank
    n_assign = t * TOP_K
    n_rows = (n_assign + tme - 1) // tme * tme + n_exp * tme
    tile_start = jnp.arange(n_rows // tme, dtype=jnp.int32) * tme
    tile_expert = jnp.minimum(jnp.sum(tile_start[:, None] >= pend[None, :], axis=1), n_exp - 1).astype(jnp.int32)
    n_used = (pend[-1] // tme).astype(jnp.int32).reshape(1)
    tok = jnp.tile(jnp.arange(t, dtype=jnp.int32), TOP_K)
    src_token = jnp.zeros((n_rows,), jnp.int32).at[dest.reshape(-1)].set(tok)
    xbuf = _gather_norm(h1_rows, g_ffn, src_token, tme)
    hbuf = _expert_up(xbuf, tile_expert, n_used, w_gate, b_gate, w_up, b_up, tme, EXPERT_COL_TILE)
    ybuf = _expert_down(hbuf, tile_expert, n_used, w_down, b_down, tme, EXPERT_COL_TILE)
    return (_combine(h1, probs, dest, ybuf, g_out, final_norm, 0, tp),
            _combine(h1, probs, dest, ybuf, g_out, final_norm, tp, t - tp))


def _conv_tail(conv_past, xb, row0, n_streams, length):
    keep = CONV_W - 1
    x3 = xb[row0:row0 + n_streams * length].reshape(n_streams, length, xb.shape[1])
    if length >= keep:
        return x3[:, length - keep:]
    return jnp.concatenate([conv_past[:, length:], x3], axis=1)


def _layer(x_src, geom, cc, ss, past_ckv, past_krope, conv_s, lru_s, lw, g_out, final_norm):
    (g_mix, w_in, g_q, w_uq, g_kv, w_uk, w_uv, w_o_attn, conv_w, conv_b, w_ra, b_ra, w_ri, b_ri,
     lam, w_o_rnn, w_out, g_ffn, w_router, b_router, w_gate, b_gate, w_up, b_up, w_down, b_down) = lw
    bp, lp, bs, ls, past = geom
    tp = bp * lp
    d = x_src[0].shape[1]
    q_lora, n_heads, qk_dim = w_uq.shape
    kv_lora, _, nope = w_uk.shape
    v_dim = w_uv.shape[2]
    rope = qk_dim - nope
    c = conv_w.shape[1]
    assert nope == LANES and v_dim == LANES and 2 * rope == LANES and tp % ls == 0

    o1, o2, o3, o4 = q_lora + kv_lora, q_lora + kv_lora + rope, q_lora + kv_lora + rope + c, q_lora + kv_lora + rope + 2 * c
    w_kr = w_in[:, o1:o2]
    w_head = jnp.concatenate([w_in[:, :o1], w_kr, _swap_halves(w_kr)], axis=1).astype(BF16)
    w_xb = w_in[:, o2:o3].astype(BF16)
    w_yb = w_in[:, o3:o4].astype(BF16)
    w_mg = w_in[:, o4:].astype(BF16)
    w_qr = w_uq[:, :, nope:]
    w_q = jnp.concatenate([w_uq[:, :, :nope], w_qr, _swap_halves(w_qr)], axis=2).reshape(q_lora, -1).astype(BF16)
    w_ukv = jnp.concatenate([w_uk.reshape(kv_lora, -1), w_uv.reshape(kv_lora, -1)], axis=1).astype(BF16)

    xn = _rmsnorm_bf16(x_src, g_mix)
    cqn, ckv, ckv_b, kr, kr_b = _head_proj(xn, w_head, g_q, g_kv, cc, ss)
    xb = _matmul([(xn, w_xb)], _ep_identity, F32, name="proj_xb")
    gy = _matmul([(xn, w_yb)], _ep_gelu, BF16, name="proj_yb")
    gates = _matmul([(xn, w_mg)], _ep_sigmoid, BF16, name="proj_gates")

    scale = float(qk_dim) ** -0.5 * math.log2(math.e)
    q = _matmul([(cqn, w_q)], functools.partial(_ep_q_rope, scale=scale), BF16, row_extras=(cc, ss), name="q_up")
    kv_p = _matmul([(ckv_b[:tp], w_ukv)], _ep_identity, BF16, name="kv_up_prompt")
    o_p = _attention(q, 0, kv_p, kr_b[:tp], bp, n_heads, lp, lp, 0)
    lk_s = past + ls
    ckv_all = jnp.concatenate([past_ckv.astype(BF16), ckv_b[tp:].reshape(bs, ls, kv_lora)], axis=1)
    kr_past = jnp.pad(past_krope, ((0, 0), (0, 0), (0, LANES - rope))).astype(BF16)
    kr_all = jnp.concatenate([kr_past, kr_b[tp:].reshape(bs, ls, LANES)], axis=1)
    o_s = _attention_absorbed(q, tp // ls, ckv_all.reshape(bs * lk_s, kv_lora), kr_all.reshape(bs * lk_s, LANES),
                              w_ukv[:, :n_heads * nope], w_ukv[:, n_heads * nope:], bs, n_heads, ls, lk_s, past)

    hg_p, lru_p = _rglru(xb, gy, 0, bp, lp, jnp.zeros((bp, CONV_W - 1, c), F32), jnp.zeros((bp, c), F32),
                         conv_w, conv_b, w_ra, b_ra, w_ri, b_ri, lam, True)
    hg_s, lru_s_new = _rglru(xb, gy, tp, bs, ls, conv_s, lru_s, conv_w, conv_b, w_ra, b_ra, w_ri, b_ri, lam, past == 0)
    conv_p_new = _conv_tail(jnp.zeros((bp, CONV_W - 1, c), F32), xb, 0, bp, lp)
    conv_s_new = _conv_tail(conv_s.astype(F32), xb, tp, bs, ls)

    mix = _matmul([((o_p, o_s), w_o_attn.astype(BF16)), ((hg_p, hg_s), w_o_rnn.astype(BF16))], _ep_merge, BF16,
                  tile_extras=((gates, 0), (gates, d)), tn=512, name="merge")
    h1, h1_rows = _matmul([(mix, w_out.astype(BF16))], _ep_residual, F32, tile_extras=((x_src, 0),),
                          row_major_copy=True, name="out_proj")

    h2 = _moe(h1, h1_rows, tp, g_ffn, w_router, b_router, w_gate, b_gate, w_up, b_up, w_down, b_down, g_out,
              final_norm)
    states_p = (ckv[:tp].reshape(bp, lp, kv_lora), kr[:tp, :rope].reshape(bp, lp, rope), conv_p_new, lru_p)
    states_s = (ckv[tp:].reshape(bs, ls, kv_lora), kr[tp:, :rope].reshape(bs, ls, rope), conv_s_new, lru_s_new)
    return h2, states_p, states_s


def kernel(x_prompt, x_sample, cache_ckv, cache_krope, state_conv, state_lru, g_mix, w_in, g_q, w_uq, g_kv, w_uk, w_uv, w_o_attn, conv_w, conv_b, w_ra, b_ra, w_ri, b_ri, lam, w_o_rnn, w_out, g_ffn, w_router, b_router, w_gate, b_gate, w_up, b_up, w_down, b_down, g_final):
    bp, lp, d = x_prompt.shape
    bs, ls, _ = x_sample.shape
    depth, _, past, _ = cache_ckv.shape
    rope = cache_krope.shape[-1]
    geom = (bp, lp, bs, ls, past)
    pos = jnp.concatenate([jnp.tile(jnp.arange(lp, dtype=jnp.int32), bp),
                           jnp.tile(past + jnp.arange(ls, dtype=jnp.int32), bs)])
    cc, ss = _rope_tables(pos, rope)
    h = (x_prompt.reshape(bp * lp, d), x_sample.reshape(bs * ls, d))
    weights = (g_mix, w_in, g_q, w_uq, g_kv, w_uk, w_uv, w_o_attn, conv_w, conv_b, w_ra, b_ra, w_ri, b_ri,
               lam, w_o_rnn, w_out, g_ffn, w_router, b_router, w_gate, b_gate, w_up, b_up, w_down, b_down)
    st_p, st_s = [], []
    for layer in range(depth):
        lw = tuple(w[layer] for w in weights)
        last = layer == depth - 1
        h, sp, ss_ = _layer(h, geom, cc, ss, cache_ckv[layer], cache_krope[layer], state_conv[layer],
                            state_lru[layer], lw, g_final if last else g_mix[layer], last)
        st_p.append(sp)
        st_s.append(ss_)
    outs = [h[0].reshape(bp, lp, d), h[1].reshape(bs, ls, d)]
    for group in (st_p, st_s):
        for k in range(4):
            outs.append(jnp.stack([s[k] for s in group]))
    return tuple(outs)
```

```python
import functools
import math

import jax
import jax.numpy as jnp
import numpy as np
from jax import lax
from jax.experimental import pallas as pl
from jax.experimental.pallas import tpu as pltpu

CHUNK = 64
ROPE_THETA = 10000.0
LRU_C = 8.0
TOP_K = 4
SWIGLU_LIMIT = 7.0
SWIGLU_ALPHA = 1.702
NORM_EPS = 1e-6
NEG_INF = -1e30

LANES = 128
SUBLANES = 8
V7X_VMEM_LIMIT_BYTES = 56 * 1024 * 1024

BF16 = jnp.bfloat16
F32 = jnp.float32


def _divisor_tile(n, pref, mult):
    if n <= pref:
        return n
    t = (pref // mult) * mult
    while t >= mult:
        if n % t == 0:
            return t
        t -= mult
    return n


def _params(sem):
    return pltpu.CompilerParams(dimension_semantics=sem, vmem_limit_bytes=V7X_VMEM_LIMIT_BYTES)


def _rms(x, g):
    return x * lax.rsqrt(jnp.mean(x * x, axis=-1, keepdims=True) + NORM_EPS) * g


def _row_sources(x):
    return tuple(x) if isinstance(x, (tuple, list)) else (x,)


def _row_tile(sources, pref, mult):
    return _divisor_tile(math.gcd(*[s.shape[0] for s in sources]), pref, mult)


def _row_specs(sources, tm, width, col_of, row_of):
    if len(sources) == 1:
        return [pl.BlockSpec((tm, width), lambda *g: (row_of(*g), col_of(*g)))]
    ntp = sources[0].shape[0] // tm
    return [pl.BlockSpec((tm, width), lambda *g: (jnp.minimum(row_of(*g), ntp - 1), col_of(*g))),
            pl.BlockSpec((tm, width), lambda *g: (jnp.maximum(row_of(*g) - ntp, 0), col_of(*g)))]


def _rmsnorm_kernel(*refs, ntp):
    g_ref, o_ref = refs[-2], refs[-1]
    if ntp is None:
        o_ref[...] = _rms(refs[0][...], g_ref[...]).astype(o_ref.dtype)
        return
    i = pl.program_id(0)

    @pl.when(i < ntp)
    def _():
        o_ref[...] = _rms(refs[0][...], g_ref[...]).astype(o_ref.dtype)

    @pl.when(i >= ntp)
    def _():
        o_ref[...] = _rms(refs[1][...], g_ref[...]).astype(o_ref.dtype)


def _rmsnorm_bf16(x, g):
    src = _row_sources(x)
    d = src[0].shape[1]
    t = sum(s.shape[0] for s in src)
    tm = _row_tile(src, 256, SUBLANES)
    ntp = src[0].shape[0] // tm if len(src) == 2 else None
    return pl.pallas_call(
        functools.partial(_rmsnorm_kernel, ntp=ntp),
        out_shape=jax.ShapeDtypeStruct((t, d), BF16),
        grid=(t // tm,),
        in_specs=_row_specs(src, tm, d, lambda i: 0, lambda i: i) + [pl.BlockSpec((1, d), lambda i: (0, 0))],
        out_specs=pl.BlockSpec((tm, d), lambda i: (i, 0)),
        compiler_params=_params(("parallel",)),
        name="rmsnorm",
    )(*src, g.reshape(1, d))


def _matmul_kernel(*refs, layout, ntp, epilogue, n_out):
    out_refs = refs[len(refs) - n_out:]

    def body(pick):
        groups, k = {"a": [], "b": [], "tile": [], "row": []}, 0
        for kind, cnt in layout:
            groups[kind].append(refs[k + (pick if cnt == 2 else 0)])
            k += cnt
        accs = [jnp.dot(a[...], b[...], preferred_element_type=F32) for a, b in zip(groups["a"], groups["b"])]
        res = epilogue(accs, [r[...] for r in groups["tile"]], [r[...] for r in groups["row"]])
        out_refs[0][...] = res.astype(out_refs[0].dtype)
        if n_out == 2:
            for c in range(out_refs[1].shape[1]):
                out_refs[1][:, c, :] = res[:, c * LANES:(c + 1) * LANES].astype(out_refs[1].dtype)

    if ntp is None:
        body(0)
        return
    i = pl.program_id(1)
    pl.when(i < ntp)(lambda: body(0))
    pl.when(i >= ntp)(lambda: body(1))


def _matmul(pairs, epilogue, out_dtype, tile_extras=(), row_extras=(), tm=512, tn=1024, row_major_copy=False,
            name="matmul"):
    all_rows = [_row_sources(a) for a, _ in pairs] + [_row_sources(e) for e, _ in tile_extras]
    all_rows += [_row_sources(e) for e in row_extras]
    m = sum(s.shape[0] for s in all_rows[0])
    n = pairs[0][1].shape[1]
    tm = _row_tile([s for src in all_rows for s in src], tm, 16)
    tn = _divisor_tile(n, tn, 2 * LANES)
    two = [src for src in all_rows if len(src) == 2]
    ntp = two[0][0].shape[0] // tm if two else None
    assert all(src[0].shape[0] == two[0][0].shape[0] for src in two)
    row_of = lambda j, i: i
    in_specs, args, layout = [], [], []
    for a, b in pairs:
        src = _row_sources(a)
        in_specs += _row_specs(src, tm, src[0].shape[1], lambda j, i: 0, row_of)
        in_specs.append(pl.BlockSpec((b.shape[0], tn), lambda j, i: (0, j)))
        args += [*src, b]
        layout += [("a", len(src)), ("b", 1)]
    for e, col0 in tile_extras:
        assert col0 % tn == 0
        src = _row_sources(e)
        in_specs += _row_specs(src, tm, tn, functools.partial(lambda j, i, jb: j + jb, jb=col0 // tn), row_of)
        args += [*src]
        layout.append(("tile", len(src)))
    for e in row_extras:
        src = _row_sources(e)
        in_specs += _row_specs(src, tm, src[0].shape[1], lambda j, i: 0, row_of)
        args += [*src]
        layout.append(("row", len(src)))
    out_shape = [jax.ShapeDtypeStruct((m, n), out_dtype)]
    out_specs = [pl.BlockSpec((tm, tn), lambda j, i: (i, j))]
    if row_major_copy:
        assert (tn // LANES) % SUBLANES == 0 or tn == n
        out_shape.append(jax.ShapeDtypeStruct((m, n // LANES, LANES), out_dtype))
        out_specs.append(pl.BlockSpec((tm, tn // LANES, LANES), lambda j, i: (i, j, 0)))
    kern = functools.partial(_matmul_kernel, layout=tuple(layout), ntp=ntp, epilogue=epilogue, n_out=len(out_shape))
    res = pl.pallas_call(
        kern,
        out_shape=tuple(out_shape),
        grid=(n // tn, m // tm),
        in_specs=in_specs,
        out_specs=tuple(out_specs),
        compiler_params=_params(("parallel", "parallel")),
        name=name,
    )(*args)
    return res if row_major_copy else res[0]


def _ep_identity(accs, tiles, rows):
    return accs[0]


def _ep_gelu(accs, tiles, rows):
    return jax.nn.gelu(accs[0])


def _ep_sigmoid(accs, tiles, rows):
    return jax.nn.sigmoid(accs[0])


def _rope_block(blk, cc, ss):
    return blk * cc + pltpu.roll(blk, LANES // 2, axis=1) * ss


def _ep_q_rope(accs, tiles, rows, *, scale):
    acc = accs[0]
    cc, ss = rows
    out = []
    for h in range(acc.shape[1] // (2 * LANES)):
        base = h * 2 * LANES
        out.append(acc[:, base:base + LANES] * scale)
        out.append(_rope_block(acc[:, base + LANES:base + 2 * LANES], cc, ss) * scale)
    return jnp.concatenate(out, axis=1)


def _ep_merge(accs, tiles, rows):
    return tiles[0].astype(F32) * accs[0] + tiles[1].astype(F32) * accs[1]


def _ep_residual(accs, tiles, rows):
    return tiles[0] + accs[0]


def _head_kernel(x_ref, w_ref, gq_ref, gkv_ref, cc_ref, ss_ref,
                 cqn_ref, ckv_ref, ckvb_ref, kr_ref, krb_ref, *, q_lora, kv_lora):
    p = jnp.dot(x_ref[...], w_ref[...], preferred_element_type=F32)
    cqn_ref[...] = _rms(p[:, :q_lora], gq_ref[...]).astype(cqn_ref.dtype)
    ckv = _rms(p[:, q_lora:q_lora + kv_lora], gkv_ref[...])
    ckv_ref[...] = ckv
    ckvb_ref[...] = ckv.astype(ckvb_ref.dtype)
    kr = _rope_block(p[:, q_lora + kv_lora:], cc_ref[...], ss_ref[...])
    kr_ref[...] = kr
    krb_ref[...] = kr.astype(krb_ref.dtype)


def _head_proj(xn, w_head, g_q, g_kv, cc, ss):
    t, d = xn.shape
    q_lora, kv_lora = g_q.shape[0], g_kv.shape[0]
    nh = w_head.shape[1]
    tm = _divisor_tile(t, 512, 16)
    row = lambda i: (i, 0)
    fixed = lambda i: (0, 0)
    return pl.pallas_call(
        functools.partial(_head_kernel, q_lora=q_lora, kv_lora=kv_lora),
        out_shape=(jax.ShapeDtypeStruct((t, q_lora), BF16),
                   jax.ShapeDtypeStruct((t, kv_lora), F32),
                   jax.ShapeDtypeStruct((t, kv_lora), BF16),
                   jax.ShapeDtypeStruct((t, LANES), F32),
                   jax.ShapeDtypeStruct((t, LANES), BF16)),
        grid=(t // tm,),
        in_specs=[pl.BlockSpec((tm, d), row), pl.BlockSpec((d, nh), fixed),
                  pl.BlockSpec((1, q_lora), fixed), pl.BlockSpec((1, kv_lora), fixed),
                  pl.BlockSpec((tm, LANES), row), pl.BlockSpec((tm, LANES), row)],
        out_specs=(pl.BlockSpec((tm, q_lora), row), pl.BlockSpec((tm, kv_lora), row),
                   pl.BlockSpec((tm, kv_lora), row), pl.BlockSpec((tm, LANES), row),
                   pl.BlockSpec((tm, LANES), row)),
        compiler_params=_params(("parallel",)),
        name="head_proj",
    )(xn, w_head, g_q.reshape(1, -1), g_kv.reshape(1, -1), cc, ss)


def _key_block_range(q_pos_first, q_pos_last, tk, nk_total):
    n_full = jnp.minimum(((q_pos_first // CHUNK + 1) * CHUNK) // tk, nk_total)
    n_need = jnp.minimum(((q_pos_last // CHUNK + 1) * CHUNK + tk - 1) // tk, nk_total)
    return n_full, n_need


def _softmax_step(s, v, m_ref, l_ref, acc_ref):
    n_s = s.shape[1] // LANES
    s_tiles = [s[:, c * LANES:(c + 1) * LANES] for c in range(n_s)]
    if s.shape[1] % LANES:
        tail = s[:, n_s * LANES:]
        s_tiles.append(jnp.concatenate([tail, jnp.full((s.shape[0], LANES - tail.shape[1]), NEG_INF, F32)], axis=1))
    tile_max = functools.reduce(jnp.maximum, s_tiles)
    m_prev = m_ref[...]
    m_new = jnp.maximum(m_prev, jnp.max(tile_max, axis=1, keepdims=True))
    alpha = jnp.exp2(m_prev - m_new)
    p_tiles = [jnp.exp2(t - m_new) for t in s_tiles]
    tile_sum = functools.reduce(jnp.add, p_tiles)
    l_ref[...] = alpha * l_ref[...] + jnp.sum(tile_sum, axis=1, keepdims=True)
    p = jnp.concatenate(p_tiles, axis=1)[:, :s.shape[1]].astype(BF16)
    pv = jnp.dot(p, v, preferred_element_type=F32)
    for c in range(acc_ref.shape[1] // LANES):
        cols = slice(c * LANES, (c + 1) * LANES)
        acc_ref[:, cols] = alpha * acc_ref[:, cols] + pv[:, cols]
    m_ref[...] = m_new


def _attn_kernel(q_ref, k_ref, kr_ref, v_ref, o_ref, m_scr, l_scr, acc_scr, *, hp, lq, lk, tq, tk, q_pos0):
    nq = lq // tq
    nk_total = lk // tk

    def q_block(qi, carry):
        r0 = pl.multiple_of(qi * tq, tq)
        m_scr[...] = jnp.full(m_scr.shape, NEG_INF, F32)
        l_scr[...] = jnp.zeros(l_scr.shape, F32)
        acc_scr[...] = jnp.zeros(acc_scr.shape, F32)
        n_full, n_need = _key_block_range(q_pos0 + r0, q_pos0 + r0 + tq - 1, tk, nk_total)

        def k_block(kj, masked):
            c0 = pl.multiple_of(kj * tk, tk)
            kr = kr_ref[pl.ds(c0, tk), :]
            if masked:
                qc = (q_pos0 + r0 + lax.broadcasted_iota(jnp.int32, (tq, tk), 0)) // CHUNK
                kc = (c0 + lax.broadcasted_iota(jnp.int32, (tq, tk), 1)) // CHUNK
                allowed = kc <= qc
            for h in range(hp):
                q = q_ref[pl.ds(r0, tq), h * 2 * LANES:(h + 1) * 2 * LANES]
                kcat = jnp.concatenate([k_ref[pl.ds(c0, tk), h * LANES:(h + 1) * LANES], kr], axis=1)
                s = lax.dot_general(q, kcat, (((1,), (1,)), ((), ())), preferred_element_type=F32)
                if masked:
                    s = jnp.where(allowed, s, NEG_INF)
                _softmax_step(s, v_ref[pl.ds(c0, tk), h * LANES:(h + 1) * LANES],
                              m_scr.at[h], l_scr.at[h], acc_scr.at[h])

        lax.fori_loop(0, n_full, lambda kj, c: (k_block(kj, False), c)[1], 0)
        lax.fori_loop(n_full, n_need, lambda kj, c: (k_block(kj, True), c)[1], 0)
        for h in range(hp):
            o_ref[pl.ds(r0, tq), h * LANES:(h + 1) * LANES] = (acc_scr[h] / l_scr[h]).astype(o_ref.dtype)
        return carry

    lax.fori_loop(0, nq, q_block, 0)


def _attention(q, q_blk0, kv, kr, n_streams, n_heads, lq, lk, q_pos0, hp=4):
    tq = _divisor_tile(lq, 512, 16)
    tk = _divisor_tile(lk, 512, 16)
    hp = hp if n_heads % hp == 0 else 1
    ng = n_heads // hp
    kern = functools.partial(_attn_kernel, hp=hp, lq=lq, lk=lk, tq=tq, tk=tk, q_pos0=q_pos0)
    return pl.pallas_call(
        kern,
        out_shape=jax.ShapeDtypeStruct((n_streams * lq, n_heads * LANES), BF16),
        grid=(n_streams, ng),
        in_specs=[pl.BlockSpec((lq, hp * 2 * LANES), lambda b, g: (q_blk0 + b, g)),
                  pl.BlockSpec((lk, hp * LANES), lambda b, g: (b, g)),
                  pl.BlockSpec((lk, LANES), lambda b, g: (b, 0)),
                  pl.BlockSpec((lk, hp * LANES), lambda b, g: (b, ng + g))],
        out_specs=pl.BlockSpec((lq, hp * LANES), lambda b, g: (b, g)),
        scratch_shapes=[pltpu.VMEM((hp, tq, LANES), F32), pltpu.VMEM((hp, tq, LANES), F32),
                        pltpu.VMEM((hp, tq, LANES), F32)],
        compiler_params=_params(("parallel", "parallel")),
        name="attention",
    )(q, kv, kr, kv)


def _attn_absorbed_kernel(q_ref, ckv_ref, kr_ref, wuk_ref, wuv_ref, o_ref, qcat_scr, m_scr, l_scr, acc_scr,
                          *, n_heads, lq, lk, tk, q_pos0):
    kvl = ckv_ref.shape[1]
    nk_total = lk // tk
    rows = n_heads * lq
    for h in range(n_heads):
        qn = q_ref[:, h * 2 * LANES:h * 2 * LANES + LANES]
        qa = lax.dot_general(qn, wuk_ref[:, h * LANES:(h + 1) * LANES], (((1,), (1,)), ((), ())),
                             preferred_element_type=F32)
        qcat_scr[h * lq:(h + 1) * lq, 0:kvl] = qa.astype(BF16)
        qcat_scr[h * lq:(h + 1) * lq, kvl:] = q_ref[:, h * 2 * LANES + LANES:(h + 1) * 2 * LANES]
    m_scr[...] = jnp.full(m_scr.shape, NEG_INF, F32)
    l_scr[...] = jnp.zeros(l_scr.shape, F32)
    acc_scr[...] = jnp.zeros(acc_scr.shape, F32)
    n_full, n_need = _key_block_range(q_pos0, q_pos0 + lq - 1, tk, nk_total)

    def k_block(kj, masked):
        c0 = pl.multiple_of(kj * tk, tk)
        ck = ckv_ref[pl.ds(c0, tk), :]
        kcat = jnp.concatenate([ck, kr_ref[pl.ds(c0, tk), :]], axis=1)
        s = lax.dot_general(qcat_scr[...], kcat, (((1,), (1,)), ((), ())), preferred_element_type=F32)
        if masked:
            qc = (q_pos0 + lax.broadcasted_iota(jnp.int32, (rows, tk), 0) % lq) // CHUNK
            kc = (c0 + lax.broadcasted_iota(jnp.int32, (rows, tk), 1)) // CHUNK
            s = jnp.where(kc <= qc, s, NEG_INF)
        _softmax_step(s, ck, m_scr, l_scr, acc_scr)

    lax.fori_loop(0, n_full, lambda kj, c: (k_block(kj, False), c)[1], 0)
    lax.fori_loop(n_full, n_need, lambda kj, c: (k_block(kj, True), c)[1], 0)
    for h in range(n_heads):
        inv_l = 1.0 / l_scr[h * lq:(h + 1) * lq, :]
        oh = jnp.concatenate([acc_scr[h * lq:(h + 1) * lq, c * LANES:(c + 1) * LANES] * inv_l
                              for c in range(kvl // LANES)], axis=1)
        o_ref[:, h * LANES:(h + 1) * LANES] = jnp.dot(
            oh.astype(BF16), wuv_ref[:, h * LANES:(h + 1) * LANES], preferred_element_type=F32).astype(o_ref.dtype)


def _attention_absorbed(q, q_blk0, ckv_all, kr_all, w_uk, w_uv, n_streams, n_heads, lq, lk, q_pos0):
    kvl = ckv_all.shape[1]
    tk = _divisor_tile(lk, 512, 16)
    rows = n_heads * lq
    kern = functools.partial(_attn_absorbed_kernel, n_heads=n_heads, lq=lq, lk=lk, tk=tk, q_pos0=q_pos0)
    fixed = lambda b: (0, 0)
    return pl.pallas_call(
        kern,
        out_shape=jax.ShapeDtypeStruct((n_streams * lq, n_heads * LANES), BF16),
        grid=(n_streams,),
        in_specs=[pl.BlockSpec((lq, n_heads * 2 * LANES), lambda b: (q_blk0 + b, 0)),
                  pl.BlockSpec((lk, kvl), lambda b: (b, 0)), pl.BlockSpec((lk, LANES), lambda b: (b, 0)),
                  pl.BlockSpec((kvl, n_heads * LANES), fixed), pl.BlockSpec((kvl, n_heads * LANES), fixed)],
        out_specs=pl.BlockSpec((lq, n_heads * LANES), lambda b: (b, 0)),
        scratch_shapes=[pltpu.VMEM((rows, kvl + LANES), BF16), pltpu.VMEM((rows, LANES), F32),
                        pltpu.VMEM((rows, LANES), F32), pltpu.VMEM((rows, kvl), F32)],
        compiler_params=_params(("parallel",)),
        name="attention_absorbed",
    )(q, ckv_all, kr_all, w_uk, w_uv)


CONV_W = 4


def _rglru_kernel(x_ref, gy_ref, past_ref, h0_ref, cw_ref, cb_ref, wra_ref, bra_ref, wri_ref, bri_ref, lam_ref,
                  hg_ref, hl_ref, xs_scr, h_scr, *, tt, reset_first):
    t = pl.program_id(2)
    nt = pl.num_programs(2)

    @pl.when(t == 0)
    def _():
        xs_scr[0:SUBLANES, :] = past_ref[...]
        h_scr[...] = h0_ref[...]

    x = x_ref[...]
    xs_scr[SUBLANES:, :] = x
    xc = cb_ref[...]
    for j in range(CONV_W):
        start = SUBLANES - (CONV_W - 1) + j
        xc = xc + xs_scr[start:start + tt, :] * cw_ref[j:j + 1, :]
    xs_scr[0:SUBLANES, :] = x[tt - SUBLANES:, :]

    xcb = xc.astype(BF16)
    r = jax.nn.sigmoid(jnp.dot(xcb, wra_ref[...], preferred_element_type=F32) + bra_ref[...])
    i = jax.nn.sigmoid(jnp.dot(xcb, wri_ref[...], preferred_element_type=F32) + bri_ref[...])
    nl = -lam_ref[...]
    softplus = jnp.maximum(nl, 0.0) + jnp.log1p(jnp.exp(-jnp.abs(nl)))
    a = jnp.exp(-LRU_C * r * softplus)
    mult = jnp.sqrt(1.0 - a * a)
    row = lax.broadcasted_iota(jnp.int32, a.shape, 0)
    if reset_first:
        first = jnp.logical_and(row == 0, t == 0)
        a = jnp.where(first, 0.0, a)
        mult = jnp.where(first, 1.0, mult)
    b = mult * i * xc

    groups = tt // SUBLANES
    a3 = a.reshape(groups, SUBLANES, a.shape[1])
    b3 = b.reshape(groups, SUBLANES, b.shape[1])
    sub = lax.broadcasted_iota(jnp.int32, a3.shape, 1)
    d = 1
    while d < SUBLANES:
        valid = sub >= d
        b3 = jnp.where(valid, a3 * pltpu.roll(b3, d, axis=1) + b3, b3)
        a3 = jnp.where(valid, a3 * pltpu.roll(a3, d, axis=1), a3)
        d *= 2
    state = h_scr[...]
    h_groups = []
    for k in range(groups):
        hk = a3[k] * state + b3[k]
        h_groups.append(hk)
        state = hk[SUBLANES - 1:, :]
    h = jnp.concatenate(h_groups, axis=0)
    h_scr[...] = state
    hg_ref[...] = (h * gy_ref[...].astype(F32)).astype(hg_ref.dtype)

    @pl.when(t == nt - 1)
    def _():
        hl_ref[...] = h[tt - 1:tt, :]


def _rglru(xb, gy, row0, n_streams, length, conv_past, h0, conv_w, conv_b, w_ra, b_ra, w_ri, b_ri, lam, reset_first):
    c = xb.shape[1]
    nb, bw, _ = w_ra.shape
    tt = _divisor_tile(length, 256, SUBLANES)
    assert tt >= SUBLANES and (tt & (tt - 1)) == 0 and row0 % tt == 0
    nt = length // tt
    blk0 = row0 // tt
    past8 = jnp.concatenate(
        [jnp.zeros((n_streams, SUBLANES - (CONV_W - 1), c), F32), conv_past.astype(F32)], axis=1)
    vec = lambda v: v.reshape(1, c).astype(F32)
    rows = lambda b, j, t: (blk0 + b * nt + t, j)
    chan = lambda b, j, t: (0, j)
    state = lambda b, j, t: (b, 0, j)
    wblk = lambda b, j, t: (j, 0, 0)
    hg, h_last = pl.pallas_call(
        functools.partial(_rglru_kernel, tt=tt, reset_first=reset_first),
        out_shape=(jax.ShapeDtypeStruct((n_streams * length, c), BF16),
                   jax.ShapeDtypeStruct((n_streams, 1, c), F32)),
        grid=(n_streams, nb, nt),
        in_specs=[pl.BlockSpec((tt, bw), rows), pl.BlockSpec((tt, bw), rows),
                  pl.BlockSpec((None, SUBLANES, bw), state), pl.BlockSpec((None, 1, bw), state),
                  pl.BlockSpec((CONV_W, bw), chan), pl.BlockSpec((1, bw), chan),
                  pl.BlockSpec((None, bw, bw), wblk), pl.BlockSpec((1, bw), chan),
                  pl.BlockSpec((None, bw, bw), wblk), pl.BlockSpec((1, bw), chan),
                  pl.BlockSpec((1, bw), chan)],
        out_specs=(pl.BlockSpec((tt, bw), lambda b, j, t: (b * nt + t, j)),
                   pl.BlockSpec((None, 1, bw), state)),
        scratch_shapes=[pltpu.VMEM((tt + SUBLANES, bw), F32), pltpu.VMEM((1, bw), F32)],
        compiler_params=_params(("parallel", "parallel", "arbitrary")),
        name="rglru",
    )(xb, gy, past8, h0.reshape(n_streams, 1, c).astype(F32), conv_w.astype(F32), vec(conv_b),
      w_ra.astype(BF16), vec(b_ra), w_ri.astype(BF16), vec(b_ri), vec(lam))
    return hg, h_last.reshape(n_streams, c)


def _router_kernel(h_ref, g_ref, wr_ref, br_ref, topi_ref, prob_ref, rank_ref, cnt_ref, carry_scr, *, n_exp, tm):
    step = pl.program_id(0)

    @pl.when(step == 0)
    def _():
        carry_scr[...] = jnp.zeros(carry_scr.shape, F32)

    hn = _rms(h_ref[...], g_ref[...])
    logits = lax.dot_general(wr_ref[...], hn, (((1,), (1,)), ((), ())),
                             precision=lax.Precision.HIGHEST, preferred_element_type=F32) + br_ref[...]
    eidx = lax.broadcasted_iota(jnp.int32, (n_exp, tm), 0)
    vals = logits
    top_v, top_i = [], []
    for _ in range(TOP_K):
        m = jnp.max(vals, axis=0, keepdims=True)
        sel = jnp.min(jnp.where(vals == m, eidx, n_exp), axis=0, keepdims=True)
        top_v.append(m)
        top_i.append(sel)
        vals = jnp.where(eidx == sel, -jnp.inf, vals)
    ex = [jnp.exp(v - top_v[0]) for v in top_v]
    denom = ex[0]
    for e in ex[1:]:
        denom = denom + e
    onehot = jnp.zeros((n_exp, tm), F32)
    for sel in top_i:
        onehot = onehot + (eidx == sel).astype(F32)
    upper = (lax.broadcasted_iota(jnp.int32, (tm, tm), 0) < lax.broadcasted_iota(jnp.int32, (tm, tm), 1))
    before = jnp.dot(onehot.astype(BF16), upper.astype(BF16), preferred_element_type=F32) + carry_scr[...]
    for k in range(TOP_K):
        topi_ref[k:k + 1, :] = top_i[k]
        prob_ref[k:k + 1, :] = ex[k] / denom
        rank_ref[k:k + 1, :] = jnp.sum(jnp.where(eidx == top_i[k], before, 0.0), axis=0,
                                       keepdims=True).astype(jnp.int32)
    carry_scr[...] = carry_scr[...] + jnp.sum(onehot, axis=1, keepdims=True)
    cnt_ref[...] = carry_scr[...].astype(jnp.int32)


def _router(h1, g_ffn, w_router, b_router):
    t, d = h1.shape
    n_exp = w_router.shape[1]
    tm = _divisor_tile(t, 512, LANES)
    kt = lambda i: (0, i)
    fixed = lambda i: (0, 0)
    return pl.pallas_call(
        functools.partial(_router_kernel, n_exp=n_exp, tm=tm),
        out_shape=(jax.ShapeDtypeStruct((TOP_K, t), jnp.int32), jax.ShapeDtypeStruct((TOP_K, t), F32),
                   jax.ShapeDtypeStruct((TOP_K, t), jnp.int32), jax.ShapeDtypeStruct((n_exp, 1), jnp.int32)),
        grid=(t // tm,),
        in_specs=[pl.BlockSpec((tm, d), lambda i: (i, 0)), pl.BlockSpec((1, d), fixed),
                  pl.BlockSpec((n_exp, d), fixed), pl.BlockSpec((n_exp, 1), fixed)],
        out_specs=(pl.BlockSpec((TOP_K, tm), kt), pl.BlockSpec((TOP_K, tm), kt), pl.BlockSpec((TOP_K, tm), kt),
                   pl.BlockSpec((n_exp, 1), fixed)),
        scratch_shapes=[pltpu.VMEM((n_exp, 1), F32)],
        compiler_params=_params(("arbitrary",)),
        name="router",
    )(h1, g_ffn.reshape(1, d), w_router.T.astype(F32), b_router.reshape(n_exp, 1).astype(F32))


GATHER_UNROLL = 8
NORM_ROW_CHUNK = 16


def _gather_norm_kernel(src_ref, nxt_ref, h_hbm, g_ref, o_ref, buf, sem, *, rows, n_piece):
    i = pl.program_id(0)
    nt = pl.num_programs(0)
    slot = i % 2

    pitch = n_piece + SUBLANES

    def row_copy(idx_ref, s, r):
        src0 = pl.multiple_of(idx_ref[0, r] * n_piece, n_piece)
        dst0 = pl.multiple_of(r * pitch, SUBLANES)
        return pltpu.make_async_copy(h_hbm.at[pl.ds(src0, n_piece), :], buf.at[s, pl.ds(dst0, n_piece), :],
                                     sem.at[s])

    def issue(idx_ref, s):
        def start_pair(q, c):
            row_copy(idx_ref, s, 2 * q).start(priority=0)
            row_copy(idx_ref, s, 2 * q + 1).start(priority=1)
            return c
        lax.fori_loop(0, rows // 2, start_pair, 0, unroll=GATHER_UNROLL // 2)

    @pl.when(i == 0)
    def _():
        issue(src_ref, 0)

    @pl.when(i + 1 < nt)
    def _():
        issue(nxt_ref, 1 - slot)

    lax.fori_loop(0, rows, lambda r, c: (row_copy(src_ref, slot, r).wait(), c)[1], 0, unroll=GATHER_UNROLL)
    for c0 in range(0, rows, NORM_ROW_CHUNK):
        nr = min(NORM_ROW_CHUNK, rows - c0)
        x = jnp.concatenate([buf[slot, pl.ds(c0 * pitch + p, nr, stride=pitch), :] for p in range(n_piece)], axis=1)
        o_ref[c0:c0 + nr, :] = _rms(x, g_ref[...]).astype(o_ref.dtype)


def _gather_norm(h1_rows, g_ffn, src_token, rows):
    t, n_piece, _ = h1_rows.shape
    d = n_piece * LANES
    h1_rows = h1_rows.reshape(t * n_piece, LANES)
    n_rows = src_token.shape[0]
    nt = n_rows // rows
    src_tiles = src_token.reshape(nt, 1, rows)
    return pl.pallas_call(
        functools.partial(_gather_norm_kernel, rows=rows, n_piece=n_piece),
        out_shape=jax.ShapeDtypeStruct((n_rows, d), BF16),
        grid=(nt,),
        in_specs=[pl.BlockSpec((None, 1, rows), lambda i: (i, 0, 0), memory_space=pltpu.SMEM),
                  pl.BlockSpec((None, 1, rows), lambda i: (jnp.minimum(i + 1, nt - 1), 0, 0),
                               memory_space=pltpu.SMEM),
                  pl.BlockSpec(memory_space=pl.ANY), pl.BlockSpec((1, d), lambda i: (0, 0))],
        out_specs=pl.BlockSpec((rows, d), lambda i: (i, 0)),
        scratch_shapes=[pltpu.VMEM((2, rows * (n_piece + SUBLANES), LANES), F32), pltpu.SemaphoreType.DMA((2,))],
        compiler_params=_params(("arbitrary",)),
        name="gather_norm",
    )(src_tiles, src_tiles, h1_rows, g_ffn.reshape(1, d))


def _by_valid_rows(valid, tm, o_ref, compute):
    half = tm // 2

    @pl.when(valid > half)
    def _():
        o_ref[...] = compute(tm).astype(o_ref.dtype)

    @pl.when(jnp.logical_and(valid > 0, valid <= half))
    def _():
        o_ref[:half, :] = compute(half).astype(o_ref.dtype)
        o_ref[half:, :] = jnp.zeros((tm - half, o_ref.shape[1]), o_ref.dtype)

    @pl.when(valid == 0)
    def _():
        o_ref[...] = jnp.zeros(o_ref.shape, o_ref.dtype)


def _expert_up_kernel(te_ref, nv_ref, x_ref, wg_ref, bg_ref, wu_ref, bu_ref, o_ref):
    def compute(rows):
        x = x_ref[:rows, :]
        g = jnp.dot(x, wg_ref[...].astype(BF16), preferred_element_type=F32) + bg_ref[...]
        u = jnp.dot(x, wu_ref[...].astype(BF16), preferred_element_type=F32) + bu_ref[...]
        g = jnp.minimum(g, SWIGLU_LIMIT)
        u = jnp.clip(u, -SWIGLU_LIMIT, SWIGLU_LIMIT)
        return (u + 1.0) * (g * jax.nn.sigmoid(SWIGLU_ALPHA * g))

    _by_valid_rows(nv_ref[pl.program_id(1)], x_ref.shape[0], o_ref, compute)


def _expert_down_kernel(te_ref, nv_ref, x_ref, wd_ref, bd_ref, o_ref):
    def compute(rows):
        return jnp.dot(x_ref[:rows, :], wd_ref[...].astype(BF16), preferred_element_type=F32) + bd_ref[...]

    _by_valid_rows(nv_ref[pl.program_id(1)], x_ref.shape[0], o_ref, compute)


def _expert_up(xbuf, tile_expert, tile_valid, w_gate, b_gate, w_up, b_up, tm, tn):
    n_rows, d = xbuf.shape
    n_exp, _, d_ff = w_gate.shape
    tn = _divisor_tile(d_ff, tn, 2 * LANES)
    wmap = lambda j, i, te, nu: (te[i], 0, j)
    return pl.pallas_call(
        _expert_up_kernel,
        out_shape=jax.ShapeDtypeStruct((n_rows, d_ff), BF16),
        grid_spec=pltpu.PrefetchScalarGridSpec(
            num_scalar_prefetch=2,
            grid=(d_ff // tn, n_rows // tm),
            in_specs=[pl.BlockSpec((tm, d), lambda j, i, te, nu: (i, 0)),
                      pl.BlockSpec((None, d, tn), wmap), pl.BlockSpec((None, 1, tn), wmap),
                      pl.BlockSpec((None, d, tn), wmap), pl.BlockSpec((None, 1, tn), wmap)],
            out_specs=pl.BlockSpec((tm, tn), lambda j, i, te, nu: (i, j))),
        compiler_params=_params(("parallel", "arbitrary")),
        name="expert_up",
    )(tile_expert, tile_valid, xbuf, w_gate, b_gate.reshape(n_exp, 1, d_ff), w_up, b_up.reshape(n_exp, 1, d_ff))


def _expert_down(hbuf, tile_expert, tile_valid, w_down, b_down, tm, tn):
    n_rows, d_ff = hbuf.shape
    n_exp, _, d = w_down.shape
    tn = _divisor_tile(d, tn, 2 * LANES)
    wmap = lambda j, i, te, nu: (te[i], 0, j)
    return pl.pallas_call(
        _expert_down_kernel,
        out_shape=jax.ShapeDtypeStruct((n_rows, d), F32),
        grid_spec=pltpu.PrefetchScalarGridSpec(
            num_scalar_prefetch=2,
            grid=(d // tn, n_rows // tm),
            in_specs=[pl.BlockSpec((tm, d_ff), lambda j, i, te, nu: (i, 0)),
                      pl.BlockSpec((None, d_ff, tn), wmap), pl.BlockSpec((None, 1, tn), wmap)],
            out_specs=pl.BlockSpec((tm, tn), lambda j, i, te, nu: (i, j))),
        compiler_params=_params(("parallel", "arbitrary")),
        name="expert_down",
    )(tile_expert, tile_valid, hbuf, w_down, b_down.reshape(n_exp, 1, d))


def _combine_kernel(dest_ref, nxt_ref, h_ref, p_ref, y_hbm, g_ref, o_ref, buf, sem, *, tm, final_norm):
    i = pl.program_id(0)
    nt = pl.num_programs(0)
    slot = i % 2

    def issue(idx_ref, s):
        def start(r, c):
            for k in range(TOP_K):
                pltpu.make_async_copy(y_hbm.at[pl.ds(idx_ref[k, r], 1), :], buf.at[s, pl.ds(k * tm + r, 1), :],
                                      sem.at[s]).start(priority=k % 2)
            return c
        lax.fori_loop(0, tm, start, 0, unroll=GATHER_UNROLL // 2)

    @pl.when(i == 0)
    def _():
        issue(dest_ref, 0)

    @pl.when(i + 1 < nt)
    def _():
        issue(nxt_ref, 1 - slot)

    pltpu.make_async_copy(y_hbm.at[pl.ds(0, TOP_K * tm), :], buf.at[slot], sem.at[slot]).wait()
    acc = h_ref[...]
    for k in range(TOP_K):
        acc = acc + buf[slot, k * tm:(k + 1) * tm, :] * p_ref[:, k:k + 1]
    o_ref[...] = _rms(acc, g_ref[...]) if final_norm else acc


def _combine(h1, probs_kt, dest_kt, ybuf, g_final, final_norm, row0, n_tok, tm=128):
    t, d = h1.shape
    tm = _divisor_tile(math.gcd(row0, n_tok) if row0 else n_tok, tm, SUBLANES)
    nt, blk0 = n_tok // tm, row0 // tm
    dest_tiles = dest_kt[:, row0:row0 + n_tok].reshape(TOP_K, nt, tm).transpose(1, 0, 2)
    return pl.pallas_call(
        functools.partial(_combine_kernel, tm=tm, final_norm=final_norm),
        out_shape=jax.ShapeDtypeStruct((n_tok, d), F32),
        grid=(nt,),
        in_specs=[pl.BlockSpec((None, TOP_K, tm), lambda i: (i, 0, 0), memory_space=pltpu.SMEM),
                  pl.BlockSpec((None, TOP_K, tm), lambda i: (jnp.minimum(i + 1, nt - 1), 0, 0),
                               memory_space=pltpu.SMEM),
                  pl.BlockSpec((tm, d), lambda i: (blk0 + i, 0)), pl.BlockSpec((tm, TOP_K), lambda i: (i, 0)),
                  pl.BlockSpec(memory_space=pl.ANY), pl.BlockSpec((1, d), lambda i: (0, 0))],
        out_specs=pl.BlockSpec((tm, d), lambda i: (i, 0)),
        scratch_shapes=[pltpu.VMEM((2, TOP_K * tm, d), F32), pltpu.SemaphoreType.DMA((2,))],
        compiler_params=_params(("arbitrary",)),
        name="combine",
    )(dest_tiles, dest_tiles, h1, probs_kt[:, row0:row0 + n_tok].T, ybuf, g_final.reshape(1, d))


EXPERT_ROW_TILE = 512
EXPERT_COL_TILE = 512


def _rope_tables(pos, rope_dim):
    half = rope_dim // 2
    inv = ROPE_THETA ** (-jnp.arange(half, dtype=F32) / half)
    ang = pos.astype(F32)[:, None] * inv[None, :]
    cos, sin = jnp.cos(ang), jnp.sin(ang)
    zeros = jnp.zeros((pos.shape[0], LANES - rope_dim), F32)
    return jnp.concatenate([cos, cos, zeros], axis=1), jnp.concatenate([-sin, sin, zeros], axis=1)


def _swap_halves(w):
    half = w.shape[-1] // 2
    return jnp.concatenate([w[..., half:], w[..., :half]], axis=-1)


def _moe(h1, h1_rows, tp, g_ffn, w_router, b_router, w_gate, b_gate, w_up, b_up, w_down, b_down, g_out,
         final_norm):
    t, d = h1.shape
    n_exp = w_router.shape[1]
    tme = EXPERT_ROW_TILE
    top_i, probs, rank, counts = _router(h1, g_ffn, w_router, b_router)
    counts = counts.reshape(n_exp)
    padded = (counts + tme - 1) // tme * tme
    pend = jnp.cumsum(padded)
    pstart = pend - padded
    first_row = jnp.sum(jnp.where(top_i[..., None] == jnp.arange(n_exp, dtype=jnp.int32), pstart, 0), axis=-1)
    dest = first_row.astype(jnp.int32) + rank
    n_assign = t * TOP_K
    n_rows = (n_assign + tme - 1) // tme * tme + n_exp * tme
    tile_start = jnp.arange(n_rows // tme, dtype=jnp.int32) * tme
    tile_expert = jnp.minimum(jnp.sum(tile_start[:, None] >= pend[None, :], axis=1), n_exp - 1).astype(jnp.int32)
    seg_end = (pstart + counts)[tile_expert]
    tile_valid = jnp.clip(seg_end - tile_start, 0, tme).astype(jnp.int32)
    tok = jnp.tile(jnp.arange(t, dtype=jnp.int32), TOP_K)
    src_token = jnp.zeros((n_rows,), jnp.int32).at[dest.reshape(-1)].set(tok)
    xbuf = _gather_norm(h1_rows, g_ffn, src_token, tme)
    hbuf = _expert_up(xbuf, tile_expert, tile_valid, w_gate, b_gate, w_up, b_up, tme, EXPERT_COL_TILE)
    ybuf = _expert_down(hbuf, tile_expert, tile_valid, w_down, b_down, tme, EXPERT_COL_TILE)
    return (_combine(h1, probs, dest, ybuf, g_out, final_norm, 0, tp),
            _combine(h1, probs, dest, ybuf, g_out, final_norm, tp, t - tp))


def _conv_tail(conv_past, xb, row0, n_streams, length):
    keep = CONV_W - 1
    if length >= keep:
        ends = [row0 + (s + 1) * length for s in range(n_streams)]
        return jnp.stack([xb[e - keep:e] for e in ends])
    x3 = xb[row0:row0 + n_streams * length].reshape(n_streams, length, xb.shape[1])
    return jnp.concatenate([conv_past[:, length:], x3], axis=1)


def _layer(x_src, geom, cc, ss, past_ckv, past_krope, conv_s, lru_s, lw, g_out, final_norm):
    (g_mix, w_in, g_q, w_uq, g_kv, w_uk, w_uv, w_o_attn, conv_w, conv_b, w_ra, b_ra, w_ri, b_ri,
     lam, w_o_rnn, w_out, g_ffn, w_router, b_router, w_gate, b_gate, w_up, b_up, w_down, b_down) = lw
    bp, lp, bs, ls, past = geom
    tp = bp * lp
    d = x_src[0].shape[1]
    q_lora, n_heads, qk_dim = w_uq.shape
    kv_lora, _, nope = w_uk.shape
    v_dim = w_uv.shape[2]
    rope = qk_dim - nope
    c = conv_w.shape[1]
    assert nope == LANES and v_dim == LANES and 2 * rope == LANES and tp % ls == 0

    o1, o2, o3, o4 = q_lora + kv_lora, q_lora + kv_lora + rope, q_lora + kv_lora + rope + c, q_lora + kv_lora + rope + 2 * c
    w_kr = w_in[:, o1:o2]
    w_head = jnp.concatenate([w_in[:, :o1], w_kr, _swap_halves(w_kr)], axis=1).astype(BF16)
    w_xb = w_in[:, o2:o3].astype(BF16)
    w_yb = w_in[:, o3:o4].astype(BF16)
    w_mg = w_in[:, o4:].astype(BF16)
    w_qr = w_uq[:, :, nope:]
    w_q = jnp.concatenate([w_uq[:, :, :nope], w_qr, _swap_halves(w_qr)], axis=2).reshape(q_lora, -1).astype(BF16)
    w_ukv = jnp.concatenate([w_uk.reshape(kv_lora, -1), w_uv.reshape(kv_lora, -1)], axis=1).astype(BF16)

    xn = _rmsnorm_bf16(x_src, g_mix)
    cqn, ckv, ckv_b, kr, kr_b = _head_proj(xn, w_head, g_q, g_kv, cc, ss)
    xb = _matmul([(xn, w_xb)], _ep_identity, F32, name="proj_xb")
    gy = _matmul([(xn, w_yb)], _ep_gelu, BF16, name="proj_yb")
    gates = _matmul([(xn, w_mg)], _ep_sigmoid, BF16, name="proj_gates")

    scale = float(qk_dim) ** -0.5 * math.log2(math.e)
    q = _matmul([(cqn, w_q)], functools.partial(_ep_q_rope, scale=scale), BF16, row_extras=(cc, ss), name="q_up")
    kv_p = _matmul([(ckv_b[:tp], w_ukv)], _ep_identity, BF16, name="kv_up_prompt")
    o_p = _attention(q, 0, kv_p, kr_b[:tp], bp, n_heads, lp, lp, 0)
    lk_s = past + ls
    ckv_all = jnp.concatenate([past_ckv.astype(BF16), ckv_b[tp:].reshape(bs, ls, kv_lora)], axis=1)
    kr_past = jnp.pad(past_krope, ((0, 0), (0, 0), (0, LANES - rope))).astype(BF16)
    kr_all = jnp.concatenate([kr_past, kr_b[tp:].reshape(bs, ls, LANES)], axis=1)
    o_s = _attention_absorbed(q, tp // ls, ckv_all.reshape(bs * lk_s, kv_lora), kr_all.reshape(bs * lk_s, LANES),
                              w_ukv[:, :n_heads * nope], w_ukv[:, n_heads * nope:], bs, n_heads, ls, lk_s, past)

    hg_p, lru_p = _rglru(xb, gy, 0, bp, lp, jnp.zeros((bp, CONV_W - 1, c), F32), jnp.zeros((bp, c), F32),
                         conv_w, conv_b, w_ra, b_ra, w_ri, b_ri, lam, True)
    hg_s, lru_s_new = _rglru(xb, gy, tp, bs, ls, conv_s, lru_s, conv_w, conv_b, w_ra, b_ra, w_ri, b_ri, lam, past == 0)
    conv_p_new = _conv_tail(jnp.zeros((bp, CONV_W - 1, c), F32), xb, 0, bp, lp)
    conv_s_new = _conv_tail(conv_s.astype(F32), xb, tp, bs, ls)

    mix = _matmul([((o_p, o_s), w_o_attn.astype(BF16)), ((hg_p, hg_s), w_o_rnn.astype(BF16))], _ep_merge, BF16,
                  tile_extras=((gates, 0), (gates, d)), tn=512, name="merge")
    h1, h1_rows = _matmul([(mix, w_out.astype(BF16))], _ep_residual, F32, tile_extras=((x_src, 0),),
                          row_major_copy=True, name="out_proj")

    h2 = _moe(h1, h1_rows, tp, g_ffn, w_router, b_router, w_gate, b_gate, w_up, b_up, w_down, b_down, g_out,
              final_norm)
    states_p = (ckv[:tp].reshape(bp, lp, kv_lora), kr[:tp, :rope].reshape(bp, lp, rope), conv_p_new, lru_p)
    states_s = (ckv[tp:].reshape(bs, ls, kv_lora), kr[tp:, :rope].reshape(bs, ls, rope), conv_s_new, lru_s_new)
    return h2, states_p, states_s


def kernel(x_prompt, x_sample, cache_ckv, cache_krope, state_conv, state_lru, g_mix, w_in, g_q, w_uq, g_kv, w_uk, w_uv, w_o_attn, conv_w, conv_b, w_ra, b_ra, w_ri, b_ri, lam, w_o_rnn, w_out, g_ffn, w_router, b_router, w_gate, b_gate, w_up, b_up, w_down, b_down, g_final):
    bp, lp, d = x_prompt.shape
    bs, ls, _ = x_sample.shape
    depth, _, past, _ = cache_ckv.shape
    rope = cache_krope.shape[-1]
    geom = (bp, lp, bs, ls, past)
    pos = jnp.concatenate([jnp.tile(jnp.arange(lp, dtype=jnp.int32), bp),
                           jnp.tile(past + jnp.arange(ls, dtype=jnp.int32), bs)])
    cc, ss = _rope_tables(pos, rope)
    h = (x_prompt.reshape(bp * lp, d), x_sample.reshape(bs * ls, d))
    weights = (g_mix, w_in, g_q, w_uq, g_kv, w_uk, w_uv, w_o_attn, conv_w, conv_b, w_ra, b_ra, w_ri, b_ri,
               lam, w_o_rnn, w_out, g_ffn, w_router, b_router, w_gate, b_gate, w_up, b_up, w_down, b_down)
    st_p, st_s = [], []
    for layer in range(depth):
        lw = tuple(w[layer] for w in weights)
        last = layer == depth - 1
        h, sp, ss_ = _layer(h, geom, cc, ss, cache_ckv[layer], cache_krope[layer], state_conv[layer],
                            state_lru[layer], lw, g_final if last else g_mix[layer], last)
        st_p.append(sp)
        st_s.append(ss_)
    outs = [h[0].reshape(bp, lp, d), h[1].reshape(bs, ls, d)]
    for group in (st_p, st_s):
        for k in range(4):
            outs.append(jnp.stack([s[k] for s in group]))
    return tuple(outs)
```

```python
import functools
import math

import jax
import jax.numpy as jnp
import numpy as np
from jax import lax
from jax.experimental import pallas as pl
from jax.experimental.pallas import tpu as pltpu

CHUNK = 64
ROPE_THETA = 10000.0
LRU_C = 8.0
TOP_K = 4
SWIGLU_LIMIT = 7.0
SWIGLU_ALPHA = 1.702
NORM_EPS = 1e-6
NEG_INF = -1e30

LANES = 128
SUBLANES = 8
V7X_VMEM_LIMIT_BYTES = 56 * 1024 * 1024

BF16 = jnp.bfloat16
F32 = jnp.float32


def _divisor_tile(n, pref, mult):
    if n <= pref:
        return n
    t = (pref // mult) * mult
    while t >= mult:
        if n % t == 0:
            return t
        t -= mult
    return n


def _params(sem):
    return pltpu.CompilerParams(dimension_semantics=sem, vmem_limit_bytes=V7X_VMEM_LIMIT_BYTES)


def _rms(x, g):
    return x * lax.rsqrt(jnp.mean(x * x, axis=-1, keepdims=True) + NORM_EPS) * g


def _row_sources(x):
    return tuple(x) if isinstance(x, (tuple, list)) else (x,)


def _row_tile(sources, pref, mult):
    return _divisor_tile(math.gcd(*[s.shape[0] for s in sources]), pref, mult)


def _row_specs(sources, tm, width, col_of, row_of):
    if len(sources) == 1:
        return [pl.BlockSpec((tm, width), lambda *g: (row_of(*g), col_of(*g)))]
    ntp = sources[0].shape[0] // tm
    return [pl.BlockSpec((tm, width), lambda *g: (jnp.minimum(row_of(*g), ntp - 1), col_of(*g))),
            pl.BlockSpec((tm, width), lambda *g: (jnp.maximum(row_of(*g) - ntp, 0), col_of(*g)))]


def _rmsnorm_kernel(*refs, ntp):
    g_ref, o_ref = refs[-2], refs[-1]
    if ntp is None:
        o_ref[...] = _rms(refs[0][...], g_ref[...]).astype(o_ref.dtype)
        return
    i = pl.program_id(0)

    @pl.when(i < ntp)
    def _():
        o_ref[...] = _rms(refs[0][...], g_ref[...]).astype(o_ref.dtype)

    @pl.when(i >= ntp)
    def _():
        o_ref[...] = _rms(refs[1][...], g_ref[...]).astype(o_ref.dtype)


def _rmsnorm_bf16(x, g):
    src = _row_sources(x)
    d = src[0].shape[1]
    t = sum(s.shape[0] for s in src)
    tm = _row_tile(src, 256, SUBLANES)
    ntp = src[0].shape[0] // tm if len(src) == 2 else None
    return pl.pallas_call(
        functools.partial(_rmsnorm_kernel, ntp=ntp),
        out_shape=jax.ShapeDtypeStruct((t, d), BF16),
        grid=(t // tm,),
        in_specs=_row_specs(src, tm, d, lambda i: 0, lambda i: i) + [pl.BlockSpec((1, d), lambda i: (0, 0))],
        out_specs=pl.BlockSpec((tm, d), lambda i: (i, 0)),
        compiler_params=_params(("parallel",)),
        name="rmsnorm",
    )(*src, g.reshape(1, d))


def _matmul_kernel(*refs, layout, ntp, epilogue, n_out):
    out_refs = refs[len(refs) - n_out:]

    def body(pick):
        groups, k = {"a": [], "b": [], "tile": [], "row": []}, 0
        for kind, cnt in layout:
            groups[kind].append(refs[k + (pick if cnt == 2 else 0)])
            k += cnt
        accs = [jnp.dot(a[...], b[...], preferred_element_type=F32) for a, b in zip(groups["a"], groups["b"])]
        res = epilogue(accs, [r[...] for r in groups["tile"]], [r[...] for r in groups["row"]])
        out_refs[0][...] = res.astype(out_refs[0].dtype)
        if n_out == 2:
            for c in range(out_refs[1].shape[1]):
                out_refs[1][:, c, :] = res[:, c * LANES:(c + 1) * LANES].astype(out_refs[1].dtype)

    if ntp is None:
        body(0)
        return
    i = pl.program_id(1)
    pl.when(i < ntp)(lambda: body(0))
    pl.when(i >= ntp)(lambda: body(1))


def _matmul(pairs, epilogue, out_dtype, tile_extras=(), row_extras=(), tm=512, tn=1024, row_major_copy=False,
            name="matmul"):
    all_rows = [_row_sources(a) for a, _ in pairs] + [_row_sources(e) for e, _ in tile_extras]
    all_rows += [_row_sources(e) for e in row_extras]
    m = sum(s.shape[0] for s in all_rows[0])
    n = pairs[0][1].shape[1]
    tm = _row_tile([s for src in all_rows for s in src], tm, 16)
    tn = _divisor_tile(n, tn, 2 * LANES)
    two = [src for src in all_rows if len(src) == 2]
    ntp = two[0][0].shape[0] // tm if two else None
    assert all(src[0].shape[0] == two[0][0].shape[0] for src in two)
    row_of = lambda j, i: i
    in_specs, args, layout = [], [], []
    for a, b in pairs:
        src = _row_sources(a)
        in_specs += _row_specs(src, tm, src[0].shape[1], lambda j, i: 0, row_of)
        in_specs.append(pl.BlockSpec((b.shape[0], tn), lambda j, i: (0, j)))
        args += [*src, b]
        layout += [("a", len(src)), ("b", 1)]
    for e, col0 in tile_extras:
        assert col0 % tn == 0
        src = _row_sources(e)
        in_specs += _row_specs(src, tm, tn, functools.partial(lambda j, i, jb: j + jb, jb=col0 // tn), row_of)
        args += [*src]
        layout.append(("tile", len(src)))
    for e in row_extras:
        src = _row_sources(e)
        in_specs += _row_specs(src, tm, src[0].shape[1], lambda j, i: 0, row_of)
        args += [*src]
        layout.append(("row", len(src)))
    out_shape = [jax.ShapeDtypeStruct((m, n), out_dtype)]
    out_specs = [pl.BlockSpec((tm, tn), lambda j, i: (i, j))]
    if row_major_copy:
        assert (tn // LANES) % SUBLANES == 0 or tn == n
        out_shape.append(jax.ShapeDtypeStruct((m, n // LANES, LANES), out_dtype))
        out_specs.append(pl.BlockSpec((tm, tn // LANES, LANES), lambda j, i: (i, j, 0)))
    kern = functools.partial(_matmul_kernel, layout=tuple(layout), ntp=ntp, epilogue=epilogue, n_out=len(out_shape))
    res = pl.pallas_call(
        kern,
        out_shape=tuple(out_shape),
        grid=(n // tn, m // tm),
        in_specs=in_specs,
        out_specs=tuple(out_specs),
        compiler_params=_params(("parallel", "parallel")),
        name=name,
    )(*args)
    return res if row_major_copy else res[0]


def _ep_identity(accs, tiles, rows):
    return accs[0]


def _ep_gelu(accs, tiles, rows):
    return jax.nn.gelu(accs[0])


def _ep_sigmoid(accs, tiles, rows):
    return jax.nn.sigmoid(accs[0])


def _rope_block(blk, cc, ss):
    return blk * cc + pltpu.roll(blk, LANES // 2, axis=1) * ss


def _ep_q_rope(accs, tiles, rows, *, scale):
    acc = accs[0]
    cc, ss = rows
    out = []
    for h in range(acc.shape[1] // (2 * LANES)):
        base = h * 2 * LANES
        out.append(acc[:, base:base + LANES] * scale)
        out.append(_rope_block(acc[:, base + LANES:base + 2 * LANES], cc, ss) * scale)
    return jnp.concatenate(out, axis=1)


def _ep_merge(accs, tiles, rows):
    return tiles[0].astype(F32) * accs[0] + tiles[1].astype(F32) * accs[1]


def _ep_residual(accs, tiles, rows):
    return tiles[0] + accs[0]


def _head_kernel(x_ref, w_ref, gq_ref, gkv_ref, cc_ref, ss_ref,
                 cqn_ref, ckv_ref, ckvb_ref, kr_ref, krb_ref, *, q_lora, kv_lora):
    p = jnp.dot(x_ref[...], w_ref[...], preferred_element_type=F32)
    cqn_ref[...] = _rms(p[:, :q_lora], gq_ref[...]).astype(cqn_ref.dtype)
    ckv = _rms(p[:, q_lora:q_lora + kv_lora], gkv_ref[...])
    ckv_ref[...] = ckv
    ckvb_ref[...] = ckv.astype(ckvb_ref.dtype)
    kr = _rope_block(p[:, q_lora + kv_lora:], cc_ref[...], ss_ref[...])
    kr_ref[...] = kr
    krb_ref[...] = kr.astype(krb_ref.dtype)


def _head_proj(xn, w_head, g_q, g_kv, cc, ss):
    t, d = xn.shape
    q_lora, kv_lora = g_q.shape[0], g_kv.shape[0]
    nh = w_head.shape[1]
    tm = _divisor_tile(t, 512, 16)
    row = lambda i: (i, 0)
    fixed = lambda i: (0, 0)
    return pl.pallas_call(
        functools.partial(_head_kernel, q_lora=q_lora, kv_lora=kv_lora),
        out_shape=(jax.ShapeDtypeStruct((t, q_lora), BF16),
                   jax.ShapeDtypeStruct((t, kv_lora), F32),
                   jax.ShapeDtypeStruct((t, kv_lora), BF16),
                   jax.ShapeDtypeStruct((t, LANES), F32),
                   jax.ShapeDtypeStruct((t, LANES), BF16)),
        grid=(t // tm,),
        in_specs=[pl.BlockSpec((tm, d), row), pl.BlockSpec((d, nh), fixed),
                  pl.BlockSpec((1, q_lora), fixed), pl.BlockSpec((1, kv_lora), fixed),
                  pl.BlockSpec((tm, LANES), row), pl.BlockSpec((tm, LANES), row)],
        out_specs=(pl.BlockSpec((tm, q_lora), row), pl.BlockSpec((tm, kv_lora), row),
                   pl.BlockSpec((tm, kv_lora), row), pl.BlockSpec((tm, LANES), row),
                   pl.BlockSpec((tm, LANES), row)),
        compiler_params=_params(("parallel",)),
        name="head_proj",
    )(xn, w_head, g_q.reshape(1, -1), g_kv.reshape(1, -1), cc, ss)


def _key_block_range(q_pos_first, q_pos_last, tk, nk_total):
    n_full = jnp.minimum(((q_pos_first // CHUNK + 1) * CHUNK) // tk, nk_total)
    n_need = jnp.minimum(((q_pos_last // CHUNK + 1) * CHUNK + tk - 1) // tk, nk_total)
    return n_full, n_need


def _softmax_step(s, v, m_ref, l_ref, acc_ref):
    n_s = s.shape[1] // LANES
    s_tiles = [s[:, c * LANES:(c + 1) * LANES] for c in range(n_s)]
    if s.shape[1] % LANES:
        tail = s[:, n_s * LANES:]
        s_tiles.append(jnp.concatenate([tail, jnp.full((s.shape[0], LANES - tail.shape[1]), NEG_INF, F32)], axis=1))
    tile_max = functools.reduce(jnp.maximum, s_tiles)
    m_prev = m_ref[...]
    m_new = jnp.maximum(m_prev, jnp.max(tile_max, axis=1, keepdims=True))
    alpha = jnp.exp2(m_prev - m_new)
    p_tiles = [jnp.exp2(t - m_new) for t in s_tiles]
    tile_sum = functools.reduce(jnp.add, p_tiles)
    l_ref[...] = alpha * l_ref[...] + jnp.sum(tile_sum, axis=1, keepdims=True)
    p = jnp.concatenate(p_tiles, axis=1)[:, :s.shape[1]].astype(BF16)
    pv = jnp.dot(p, v, preferred_element_type=F32)
    for c in range(acc_ref.shape[1] // LANES):
        cols = slice(c * LANES, (c + 1) * LANES)
        acc_ref[:, cols] = alpha * acc_ref[:, cols] + pv[:, cols]
    m_ref[...] = m_new


def _attn_kernel(q_ref, k_ref, kr_ref, v_ref, o_ref, m_scr, l_scr, acc_scr, *, hp, lq, lk, tq, tk, q_pos0):
    nq = lq // tq
    nk_total = lk // tk

    def q_block(qi, carry):
        r0 = pl.multiple_of(qi * tq, tq)
        m_scr[...] = jnp.full(m_scr.shape, NEG_INF, F32)
        l_scr[...] = jnp.zeros(l_scr.shape, F32)
        acc_scr[...] = jnp.zeros(acc_scr.shape, F32)
        n_full, n_need = _key_block_range(q_pos0 + r0, q_pos0 + r0 + tq - 1, tk, nk_total)

        def k_block(kj, masked):
            c0 = pl.multiple_of(kj * tk, tk)
            kr = kr_ref[pl.ds(c0, tk), :]
            if masked:
                qc = (q_pos0 + r0 + lax.broadcasted_iota(jnp.int32, (tq, tk), 0)) // CHUNK
                kc = (c0 + lax.broadcasted_iota(jnp.int32, (tq, tk), 1)) // CHUNK
                allowed = kc <= qc
            for h in range(hp):
                q = q_ref[pl.ds(r0, tq), h * 2 * LANES:(h + 1) * 2 * LANES]
                kcat = jnp.concatenate([k_ref[pl.ds(c0, tk), h * LANES:(h + 1) * LANES], kr], axis=1)
                s = lax.dot_general(q, kcat, (((1,), (1,)), ((), ())), preferred_element_type=F32)
                if masked:
                    s = jnp.where(allowed, s, NEG_INF)
                _softmax_step(s, v_ref[pl.ds(c0, tk), h * LANES:(h + 1) * LANES],
                              m_scr.at[h], l_scr.at[h], acc_scr.at[h])

        lax.fori_loop(0, n_full, lambda kj, c: (k_block(kj, False), c)[1], 0)
        lax.fori_loop(n_full, n_need, lambda kj, c: (k_block(kj, True), c)[1], 0)
        for h in range(hp):
            o_ref[pl.ds(r0, tq), h * LANES:(h + 1) * LANES] = (acc_scr[h] / l_scr[h]).astype(o_ref.dtype)
        return carry

    lax.fori_loop(0, nq, q_block, 0)


def _attention(q, q_blk0, kv, kr, n_streams, n_heads, lq, lk, q_pos0, hp=4):
    tq = _divisor_tile(lq, 512, 16)
    tk = _divisor_tile(lk, 512, 16)
    hp = hp if n_heads % hp == 0 else 1
    ng = n_heads // hp
    kern = functools.partial(_attn_kernel, hp=hp, lq=lq, lk=lk, tq=tq, tk=tk, q_pos0=q_pos0)
    return pl.pallas_call(
        kern,
        out_shape=jax.ShapeDtypeStruct((n_streams * lq, n_heads * LANES), BF16),
        grid=(n_streams, ng),
        in_specs=[pl.BlockSpec((lq, hp * 2 * LANES), lambda b, g: (q_blk0 + b, g)),
                  pl.BlockSpec((lk, hp * LANES), lambda b, g: (b, g)),
                  pl.BlockSpec((lk, LANES), lambda b, g: (b, 0)),
                  pl.BlockSpec((lk, hp * LANES), lambda b, g: (b, ng + g))],
        out_specs=pl.BlockSpec((lq, hp * LANES), lambda b, g: (b, g)),
        scratch_shapes=[pltpu.VMEM((hp, tq, LANES), F32), pltpu.VMEM((hp, tq, LANES), F32),
                        pltpu.VMEM((hp, tq, LANES), F32)],
        compiler_params=_params(("parallel", "parallel")),
        name="attention",
    )(q, kv, kr, kv)


def _attn_absorbed_kernel(q_ref, ckv_ref, kr_ref, wuk_ref, wuv_ref, o_ref, qcat_scr, m_scr, l_scr, acc_scr,
                          *, n_heads, lq, lk, tk, q_pos0):
    kvl = ckv_ref.shape[1]
    nk_total = lk // tk
    rows = n_heads * lq
    for h in range(n_heads):
        qn = q_ref[:, h * 2 * LANES:h * 2 * LANES + LANES]
        qa = lax.dot_general(qn, wuk_ref[:, h * LANES:(h + 1) * LANES], (((1,), (1,)), ((), ())),
                             preferred_element_type=F32)
        qcat_scr[h * lq:(h + 1) * lq, 0:kvl] = qa.astype(BF16)
        qcat_scr[h * lq:(h + 1) * lq, kvl:] = q_ref[:, h * 2 * LANES + LANES:(h + 1) * 2 * LANES]
    m_scr[...] = jnp.full(m_scr.shape, NEG_INF, F32)
    l_scr[...] = jnp.zeros(l_scr.shape, F32)
    acc_scr[...] = jnp.zeros(acc_scr.shape, F32)
    n_full, n_need = _key_block_range(q_pos0, q_pos0 + lq - 1, tk, nk_total)

    def k_block(kj, masked):
        c0 = pl.multiple_of(kj * tk, tk)
        ck = ckv_ref[pl.ds(c0, tk), :]
        kcat = jnp.concatenate([ck, kr_ref[pl.ds(c0, tk), :]], axis=1)
        s = lax.dot_general(qcat_scr[...], kcat, (((1,), (1,)), ((), ())), preferred_element_type=F32)
        if masked:
            qc = (q_pos0 + lax.broadcasted_iota(jnp.int32, (rows, tk), 0) % lq) // CHUNK
            kc = (c0 + lax.broadcasted_iota(jnp.int32, (rows, tk), 1)) // CHUNK
            s = jnp.where(kc <= qc, s, NEG_INF)
        _softmax_step(s, ck, m_scr, l_scr, acc_scr)

    lax.fori_loop(0, n_full, lambda kj, c: (k_block(kj, False), c)[1], 0)
    lax.fori_loop(n_full, n_need, lambda kj, c: (k_block(kj, True), c)[1], 0)
    for h in range(n_heads):
        inv_l = 1.0 / l_scr[h * lq:(h + 1) * lq, :]
        oh = jnp.concatenate([acc_scr[h * lq:(h + 1) * lq, c * LANES:(c + 1) * LANES] * inv_l
                              for c in range(kvl // LANES)], axis=1)
        o_ref[:, h * LANES:(h + 1) * LANES] = jnp.dot(
            oh.astype(BF16), wuv_ref[:, h * LANES:(h + 1) * LANES], preferred_element_type=F32).astype(o_ref.dtype)


def _attention_absorbed(q, q_blk0, ckv_all, kr_all, w_uk, w_uv, n_streams, n_heads, lq, lk, q_pos0):
    kvl = ckv_all.shape[1]
    tk = _divisor_tile(lk, 512, 16)
    rows = n_heads * lq
    kern = functools.partial(_attn_absorbed_kernel, n_heads=n_heads, lq=lq, lk=lk, tk=tk, q_pos0=q_pos0)
    fixed = lambda b: (0, 0)
    return pl.pallas_call(
        kern,
        out_shape=jax.ShapeDtypeStruct((n_streams * lq, n_heads * LANES), BF16),
        grid=(n_streams,),
        in_specs=[pl.BlockSpec((lq, n_heads * 2 * LANES), lambda b: (q_blk0 + b, 0)),
                  pl.BlockSpec((lk, kvl), lambda b: (b, 0)), pl.BlockSpec((lk, LANES), lambda b: (b, 0)),
                  pl.BlockSpec((kvl, n_heads * LANES), fixed), pl.BlockSpec((kvl, n_heads * LANES), fixed)],
        out_specs=pl.BlockSpec((lq, n_heads * LANES), lambda b: (b, 0)),
        scratch_shapes=[pltpu.VMEM((rows, kvl + LANES), BF16), pltpu.VMEM((rows, LANES), F32),
                        pltpu.VMEM((rows, LANES), F32), pltpu.VMEM((rows, kvl), F32)],
        compiler_params=_params(("parallel",)),
        name="attention_absorbed",
    )(q, ckv_all, kr_all, w_uk, w_uv)


CONV_W = 4


def _rglru_kernel(x_ref, gy_ref, past_ref, h0_ref, cw_ref, cb_ref, wra_ref, bra_ref, wri_ref, bri_ref, lam_ref,
                  hg_ref, hl_ref, xs_scr, h_scr, *, tt, reset_first):
    t = pl.program_id(2)
    nt = pl.num_programs(2)

    @pl.when(t == 0)
    def _():
        xs_scr[0:SUBLANES, :] = past_ref[...]
        h_scr[...] = h0_ref[...]

    x = x_ref[...]
    xs_scr[SUBLANES:, :] = x
    xc = cb_ref[...]
    for j in range(CONV_W):
        start = SUBLANES - (CONV_W - 1) + j
        xc = xc + xs_scr[start:start + tt, :] * cw_ref[j:j + 1, :]
    xs_scr[0:SUBLANES, :] = x[tt - SUBLANES:, :]

    xcb = xc.astype(BF16)
    r = jax.nn.sigmoid(jnp.dot(xcb, wra_ref[...], preferred_element_type=F32) + bra_ref[...])
    i = jax.nn.sigmoid(jnp.dot(xcb, wri_ref[...], preferred_element_type=F32) + bri_ref[...])
    nl = -lam_ref[...]
    softplus = jnp.maximum(nl, 0.0) + jnp.log1p(jnp.exp(-jnp.abs(nl)))
    a = jnp.exp(-LRU_C * r * softplus)
    mult = jnp.sqrt(1.0 - a * a)
    row = lax.broadcasted_iota(jnp.int32, a.shape, 0)
    if reset_first:
        first = jnp.logical_and(row == 0, t == 0)
        a = jnp.where(first, 0.0, a)
        mult = jnp.where(first, 1.0, mult)
    b = mult * i * xc

    groups = tt // SUBLANES
    a3 = a.reshape(groups, SUBLANES, a.shape[1])
    b3 = b.reshape(groups, SUBLANES, b.shape[1])
    sub = lax.broadcasted_iota(jnp.int32, a3.shape, 1)
    d = 1
    while d < SUBLANES:
        valid = sub >= d
        b3 = jnp.where(valid, a3 * pltpu.roll(b3, d, axis=1) + b3, b3)
        a3 = jnp.where(valid, a3 * pltpu.roll(a3, d, axis=1), a3)
        d *= 2
    state = h_scr[...]
    h_groups = []
    for k in range(groups):
        hk = a3[k] * state + b3[k]
        h_groups.append(hk)
        state = hk[SUBLANES - 1:, :]
    h = jnp.concatenate(h_groups, axis=0)
    h_scr[...] = state
    hg_ref[...] = (h * gy_ref[...].astype(F32)).astype(hg_ref.dtype)

    @pl.when(t == nt - 1)
    def _():
        hl_ref[...] = h[tt - 1:tt, :]


def _rglru(xb, gy, row0, n_streams, length, conv_past, h0, conv_w, conv_b, w_ra, b_ra, w_ri, b_ri, lam, reset_first):
    c = xb.shape[1]
    nb, bw, _ = w_ra.shape
    tt = _divisor_tile(length, 256, SUBLANES)
    assert tt >= SUBLANES and (tt & (tt - 1)) == 0 and row0 % tt == 0
    nt = length // tt
    blk0 = row0 // tt
    past8 = jnp.concatenate(
        [jnp.zeros((n_streams, SUBLANES - (CONV_W - 1), c), F32), conv_past.astype(F32)], axis=1)
    vec = lambda v: v.reshape(1, c).astype(F32)
    rows = lambda b, j, t: (blk0 + b * nt + t, j)
    chan = lambda b, j, t: (0, j)
    state = lambda b, j, t: (b, 0, j)
    wblk = lambda b, j, t: (j, 0, 0)
    hg, h_last = pl.pallas_call(
        functools.partial(_rglru_kernel, tt=tt, reset_first=reset_first),
        out_shape=(jax.ShapeDtypeStruct((n_streams * length, c), BF16),
                   jax.ShapeDtypeStruct((n_streams, 1, c), F32)),
        grid=(n_streams, nb, nt),
        in_specs=[pl.BlockSpec((tt, bw), rows), pl.BlockSpec((tt, bw), rows),
                  pl.BlockSpec((None, SUBLANES, bw), state), pl.BlockSpec((None, 1, bw), state),
                  pl.BlockSpec((CONV_W, bw), chan), pl.BlockSpec((1, bw), chan),
                  pl.BlockSpec((None, bw, bw), wblk), pl.BlockSpec((1, bw), chan),
                  pl.BlockSpec((None, bw, bw), wblk), pl.BlockSpec((1, bw), chan),
                  pl.BlockSpec((1, bw), chan)],
        out_specs=(pl.BlockSpec((tt, bw), lambda b, j, t: (b * nt + t, j)),
                   pl.BlockSpec((None, 1, bw), state)),
        scratch_shapes=[pltpu.VMEM((tt + SUBLANES, bw), F32), pltpu.VMEM((1, bw), F32)],
        compiler_params=_params(("parallel", "parallel", "arbitrary")),
        name="rglru",
    )(xb, gy, past8, h0.reshape(n_streams, 1, c).astype(F32), conv_w.astype(F32), vec(conv_b),
      w_ra.astype(BF16), vec(b_ra), w_ri.astype(BF16), vec(b_ri), vec(lam))
    return hg, h_last.reshape(n_streams, c)


def _router_kernel(h_ref, g_ref, wr_ref, br_ref, topi_ref, prob_ref, rank_ref, cnt_ref, carry_scr, *, n_exp, tm):
    step = pl.program_id(0)

    @pl.when(step == 0)
    def _():
        carry_scr[...] = jnp.zeros(carry_scr.shape, F32)

    hn = _rms(h_ref[...], g_ref[...])
    logits = lax.dot_general(wr_ref[...], hn, (((1,), (1,)), ((), ())),
                             precision=lax.Precision.HIGHEST, preferred_element_type=F32) + br_ref[...]
    eidx = lax.broadcasted_iota(jnp.int32, (n_exp, tm), 0)
    vals = logits
    top_v, top_i = [], []
    for _ in range(TOP_K):
        m = jnp.max(vals, axis=0, keepdims=True)
        sel = jnp.min(jnp.where(vals == m, eidx, n_exp), axis=0, keepdims=True)
        top_v.append(m)
        top_i.append(sel)
        vals = jnp.where(eidx == sel, -jnp.inf, vals)
    ex = [jnp.exp(v - top_v[0]) for v in top_v]
    denom = ex[0]
    for e in ex[1:]:
        denom = denom + e
    onehot = jnp.zeros((n_exp, tm), F32)
    for sel in top_i:
        onehot = onehot + (eidx == sel).astype(F32)
    upper = (lax.broadcasted_iota(jnp.int32, (tm, tm), 0) < lax.broadcasted_iota(jnp.int32, (tm, tm), 1))
    before = jnp.dot(onehot.astype(BF16), upper.astype(BF16), preferred_element_type=F32) + carry_scr[...]
    for k in range(TOP_K):
        topi_ref[k:k + 1, :] = top_i[k]
        prob_ref[k:k + 1, :] = ex[k] / denom
        rank_ref[k:k + 1, :] = jnp.sum(jnp.where(eidx == top_i[k], before, 0.0), axis=0,
                                       keepdims=True).astype(jnp.int32)
    carry_scr[...] = carry_scr[...] + jnp.sum(onehot, axis=1, keepdims=True)
    cnt_ref[...] = carry_scr[...].astype(jnp.int32)


def _router(h1, g_ffn, w_router, b_router):
    t, d = h1.shape
    n_exp = w_router.shape[1]
    tm = _divisor_tile(t, 512, LANES)
    kt = lambda i: (0, i)
    fixed = lambda i: (0, 0)
    return pl.pallas_call(
        functools.partial(_router_kernel, n_exp=n_exp, tm=tm),
        out_shape=(jax.ShapeDtypeStruct((TOP_K, t), jnp.int32), jax.ShapeDtypeStruct((TOP_K, t), F32),
                   jax.ShapeDtypeStruct((TOP_K, t), jnp.int32), jax.ShapeDtypeStruct((n_exp, 1), jnp.int32)),
        grid=(t // tm,),
        in_specs=[pl.BlockSpec((tm, d), lambda i: (i, 0)), pl.BlockSpec((1, d), fixed),
                  pl.BlockSpec((n_exp, d), fixed), pl.BlockSpec((n_exp, 1), fixed)],
        out_specs=(pl.BlockSpec((TOP_K, tm), kt), pl.BlockSpec((TOP_K, tm), kt), pl.BlockSpec((TOP_K, tm), kt),
                   pl.BlockSpec((n_exp, 1), fixed)),
        scratch_shapes=[pltpu.VMEM((n_exp, 1), F32)],
        compiler_params=_params(("arbitrary",)),
        name="router",
    )(h1, g_ffn.reshape(1, d), w_router.T.astype(F32), b_router.reshape(n_exp, 1).astype(F32))


GATHER_UNROLL = 8
NORM_ROW_CHUNK = 16


def _gather_norm_kernel(src_ref, nxt_ref, h_hbm, g_ref, o_ref, buf, sem, *, rows, n_piece):
    i = pl.program_id(0)
    nt = pl.num_programs(0)
    slot = i % 2

    pitch = n_piece + SUBLANES

    def row_copy(idx_ref, s, r):
        src0 = pl.multiple_of(idx_ref[0, r] * n_piece, n_piece)
        dst0 = pl.multiple_of(r * pitch, SUBLANES)
        return pltpu.make_async_copy(h_hbm.at[pl.ds(src0, n_piece), :], buf.at[s, pl.ds(dst0, n_piece), :],
                                     sem.at[s])

    def issue(idx_ref, s):
        def start_pair(q, c):
            row_copy(idx_ref, s, 2 * q).start(priority=0)
            row_copy(idx_ref, s, 2 * q + 1).start(priority=1)
            return c
        lax.fori_loop(0, rows // 2, start_pair, 0, unroll=GATHER_UNROLL // 2)

    @pl.when(i == 0)
    def _():
        issue(src_ref, 0)

    @pl.when(i + 1 < nt)
    def _():
        issue(nxt_ref, 1 - slot)

    lax.fori_loop(0, rows, lambda r, c: (row_copy(src_ref, slot, r).wait(), c)[1], 0, unroll=GATHER_UNROLL)
    for c0 in range(0, rows, NORM_ROW_CHUNK):
        nr = min(NORM_ROW_CHUNK, rows - c0)
        x = jnp.concatenate([buf[slot, pl.ds(c0 * pitch + p, nr, stride=pitch), :] for p in range(n_piece)], axis=1)
        o_ref[c0:c0 + nr, :] = _rms(x, g_ref[...]).astype(o_ref.dtype)


def _gather_norm(h1_rows, g_ffn, src_token, rows):
    t, n_piece, _ = h1_rows.shape
    d = n_piece * LANES
    h1_rows = h1_rows.reshape(t * n_piece, LANES)
    n_rows = src_token.shape[0]
    nt = n_rows // rows
    src_tiles = src_token.reshape(nt, 1, rows)
    return pl.pallas_call(
        functools.partial(_gather_norm_kernel, rows=rows, n_piece=n_piece),
        out_shape=jax.ShapeDtypeStruct((n_rows, d), BF16),
        grid=(nt,),
        in_specs=[pl.BlockSpec((None, 1, rows), lambda i: (i, 0, 0), memory_space=pltpu.SMEM),
                  pl.BlockSpec((None, 1, rows), lambda i: (jnp.minimum(i + 1, nt - 1), 0, 0),
                               memory_space=pltpu.SMEM),
                  pl.BlockSpec(memory_space=pl.ANY), pl.BlockSpec((1, d), lambda i: (0, 0))],
        out_specs=pl.BlockSpec((rows, d), lambda i: (i, 0)),
        scratch_shapes=[pltpu.VMEM((2, rows * (n_piece + SUBLANES), LANES), F32), pltpu.SemaphoreType.DMA((2,))],
        compiler_params=_params(("arbitrary",)),
        name="gather_norm",
    )(src_tiles, src_tiles, h1_rows, g_ffn.reshape(1, d))


EXPERT_SKIP_PARTS = 4


def _skip_leading_pad(lead, tm, o_ref, compute):
    part = tm // EXPERT_SKIP_PARTS
    for q in range(EXPERT_SKIP_PARTS):
        row0 = q * part

        def branch(row0=row0):
            if row0:
                o_ref[:row0, :] = jnp.zeros((row0, o_ref.shape[1]), o_ref.dtype)
            o_ref[row0:, :] = compute(row0).astype(o_ref.dtype)

        pl.when(jnp.logical_and(lead >= row0, lead < row0 + part))(branch)

    @pl.when(lead >= tm)
    def _():
        o_ref[...] = jnp.zeros(o_ref.shape, o_ref.dtype)


def _expert_up_kernel(te_ref, lead_ref, x_ref, wg_ref, bg_ref, wu_ref, bu_ref, o_ref):
    def compute(row0):
        x = x_ref[row0:, :]
        g = jnp.dot(x, wg_ref[...].astype(BF16), preferred_element_type=F32) + bg_ref[...]
        u = jnp.dot(x, wu_ref[...].astype(BF16), preferred_element_type=F32) + bu_ref[...]
        g = jnp.minimum(g, SWIGLU_LIMIT)
        u = jnp.clip(u, -SWIGLU_LIMIT, SWIGLU_LIMIT)
        return (u + 1.0) * (g * jax.nn.sigmoid(SWIGLU_ALPHA * g))

    _skip_leading_pad(lead_ref[pl.program_id(1)], x_ref.shape[0], o_ref, compute)


def _expert_down_kernel(te_ref, lead_ref, x_ref, wd_ref, bd_ref, o_ref):
    def compute(row0):
        return jnp.dot(x_ref[row0:, :], wd_ref[...].astype(BF16), preferred_element_type=F32) + bd_ref[...]

    _skip_leading_pad(lead_ref[pl.program_id(1)], x_ref.shape[0], o_ref, compute)


def _expert_up(xbuf, tile_expert, tile_lead, w_gate, b_gate, w_up, b_up, tm, tn):
    n_rows, d = xbuf.shape
    n_exp, _, d_ff = w_gate.shape
    tn = _divisor_tile(d_ff, tn, 2 * LANES)
    wmap = lambda j, i, te, nu: (te[i], 0, j)
    return pl.pallas_call(
        _expert_up_kernel,
        out_shape=jax.ShapeDtypeStruct((n_rows, d_ff), BF16),
        grid_spec=pltpu.PrefetchScalarGridSpec(
            num_scalar_prefetch=2,
            grid=(d_ff // tn, n_rows // tm),
            in_specs=[pl.BlockSpec((tm, d), lambda j, i, te, nu: (i, 0)),
                      pl.BlockSpec((None, d, tn), wmap), pl.BlockSpec((None, 1, tn), wmap),
                      pl.BlockSpec((None, d, tn), wmap), pl.BlockSpec((None, 1, tn), wmap)],
            out_specs=pl.BlockSpec((tm, tn), lambda j, i, te, nu: (i, j))),
        compiler_params=_params(("parallel", "arbitrary")),
        name="expert_up",
    )(tile_expert, tile_lead, xbuf, w_gate, b_gate.reshape(n_exp, 1, d_ff), w_up, b_up.reshape(n_exp, 1, d_ff))


def _expert_down(hbuf, tile_expert, tile_lead, w_down, b_down, tm, tn):
    n_rows, d_ff = hbuf.shape
    n_exp, _, d = w_down.shape
    tn = _divisor_tile(d, tn, 2 * LANES)
    wmap = lambda j, i, te, nu: (te[i], 0, j)
    return pl.pallas_call(
        _expert_down_kernel,
        out_shape=jax.ShapeDtypeStruct((n_rows, d), F32),
        grid_spec=pltpu.PrefetchScalarGridSpec(
            num_scalar_prefetch=2,
            grid=(d // tn, n_rows // tm),
            in_specs=[pl.BlockSpec((tm, d_ff), lambda j, i, te, nu: (i, 0)),
                      pl.BlockSpec((None, d_ff, tn), wmap), pl.BlockSpec((None, 1, tn), wmap)],
            out_specs=pl.BlockSpec((tm, tn), lambda j, i, te, nu: (i, j))),
        compiler_params=_params(("parallel", "arbitrary")),
        name="expert_down",
    )(tile_expert, tile_lead, hbuf, w_down, b_down.reshape(n_exp, 1, d))


def _combine_kernel(dest_ref, nxt_ref, h_ref, p_ref, y_hbm, g_ref, o_ref, buf, sem, *, tm, final_norm):
    i = pl.program_id(0)
    nt = pl.num_programs(0)
    slot = i % 2

    def issue(idx_ref, s):
        def start(r, c):
            for k in range(TOP_K):
                pltpu.make_async_copy(y_hbm.at[pl.ds(idx_ref[k, r], 1), :], buf.at[s, pl.ds(k * tm + r, 1), :],
                                      sem.at[s]).start(priority=k % 2)
            return c
        lax.fori_loop(0, tm, start, 0, unroll=GATHER_UNROLL // 2)

    @pl.when(i == 0)
    def _():
        issue(dest_ref, 0)

    @pl.when(i + 1 < nt)
    def _():
        issue(nxt_ref, 1 - slot)

    pltpu.make_async_copy(y_hbm.at[pl.ds(0, TOP_K * tm), :], buf.at[slot], sem.at[slot]).wait()
    acc = h_ref[...]
    for k in range(TOP_K):
        acc = acc + buf[slot, k * tm:(k + 1) * tm, :] * p_ref[:, k:k + 1]
    o_ref[...] = _rms(acc, g_ref[...]) if final_norm else acc


def _combine(h1, probs_kt, dest_kt, ybuf, g_final, final_norm, row0, n_tok, tm=128):
    t, d = h1.shape
    tm = _divisor_tile(math.gcd(row0, n_tok) if row0 else n_tok, tm, SUBLANES)
    nt, blk0 = n_tok // tm, row0 // tm
    dest_tiles = dest_kt[:, row0:row0 + n_tok].reshape(TOP_K, nt, tm).transpose(1, 0, 2)
    return pl.pallas_call(
        functools.partial(_combine_kernel, tm=tm, final_norm=final_norm),
        out_shape=jax.ShapeDtypeStruct((n_tok, d), F32),
        grid=(nt,),
        in_specs=[pl.BlockSpec((None, TOP_K, tm), lambda i: (i, 0, 0), memory_space=pltpu.SMEM),
                  pl.BlockSpec((None, TOP_K, tm), lambda i: (jnp.minimum(i + 1, nt - 1), 0, 0),
                               memory_space=pltpu.SMEM),
                  pl.BlockSpec((tm, d), lambda i: (blk0 + i, 0)), pl.BlockSpec((tm, TOP_K), lambda i: (i, 0)),
                  pl.BlockSpec(memory_space=pl.ANY), pl.BlockSpec((1, d), lambda i: (0, 0))],
        out_specs=pl.BlockSpec((tm, d), lambda i: (i, 0)),
        scratch_shapes=[pltpu.VMEM((2, TOP_K * tm, d), F32), pltpu.SemaphoreType.DMA((2,))],
        compiler_params=_params(("arbitrary",)),
        name="combine",
    )(dest_tiles, dest_tiles, h1, probs_kt[:, row0:row0 + n_tok].T, ybuf, g_final.reshape(1, d))


EXPERT_ROW_TILE = 512
EXPERT_COL_TILE = 512


def _rope_tables(pos, rope_dim):
    half = rope_dim // 2
    inv = ROPE_THETA ** (-jnp.arange(half, dtype=F32) / half)
    ang = pos.astype(F32)[:, None] * inv[None, :]
    cos, sin = jnp.cos(ang), jnp.sin(ang)
    zeros = jnp.zeros((pos.shape[0], LANES - rope_dim), F32)
    return jnp.concatenate([cos, cos, zeros], axis=1), jnp.concatenate([-sin, sin, zeros], axis=1)


def _swap_halves(w):
    half = w.shape[-1] // 2
    return jnp.concatenate([w[..., half:], w[..., :half]], axis=-1)


def _moe(h1, h1_rows, tp, g_ffn, w_router, b_router, w_gate, b_gate, w_up, b_up, w_down, b_down, g_out,
         final_norm):
    t, d = h1.shape
    n_exp = w_router.shape[1]
    tme = EXPERT_ROW_TILE
    top_i, probs, rank, counts = _router(h1, g_ffn, w_router, b_router)
    counts = counts.reshape(n_exp)
    padded = (counts + tme - 1) // tme * tme
    pend = jnp.cumsum(padded)
    seg_first = pend - counts
    first_row = jnp.sum(jnp.where(top_i[..., None] == jnp.arange(n_exp, dtype=jnp.int32), seg_first, 0), axis=-1)
    dest = first_row.astype(jnp.int32) + rank
    n_assign = t * TOP_K
    n_rows = (n_assign + tme - 1) // tme * tme + n_exp * tme
    tile_start = jnp.arange(n_rows // tme, dtype=jnp.int32) * tme
    tile_expert = jnp.minimum(jnp.sum(tile_start[:, None] >= pend[None, :], axis=1), n_exp - 1).astype(jnp.int32)
    tile_lead = jnp.where(tile_start >= pend[-1], tme,
                           jnp.clip(seg_first[tile_expert] - tile_start, 0, tme)).astype(jnp.int32)
    tok = jnp.tile(jnp.arange(t, dtype=jnp.int32), TOP_K)
    src_token = jnp.zeros((n_rows,), jnp.int32).at[dest.reshape(-1)].set(tok)
    xbuf = _gather_norm(h1_rows, g_ffn, src_token, tme)
    hbuf = _expert_up(xbuf, tile_expert, tile_lead, w_gate, b_gate, w_up, b_up, tme, EXPERT_COL_TILE)
    ybuf = _expert_down(hbuf, tile_expert, tile_lead, w_down, b_down, tme, EXPERT_COL_TILE)
    return (_combine(h1, probs, dest, ybuf, g_out, final_norm, 0, tp),
            _combine(h1, probs, dest, ybuf, g_out, final_norm, tp, t - tp))


def _conv_tail(conv_past, xb, row0, n_streams, length):
    keep = CONV_W - 1
    if length >= keep:
        ends = [row0 + (s + 1) * length for s in range(n_streams)]
        return jnp.stack([xb[e - keep:e] for e in ends])
    x3 = xb[row0:row0 + n_streams * length].reshape(n_streams, length, xb.shape[1])
    return jnp.concatenate([conv_past[:, length:], x3], axis=1)


def _layer(x_src, geom, cc, ss, past_ckv, past_krope, conv_s, lru_s, lw, g_out, final_norm):
    (g_mix, w_in, g_q, w_uq, g_kv, w_uk, w_uv, w_o_attn, conv_w, conv_b, w_ra, b_ra, w_ri, b_ri,
     lam, w_o_rnn, w_out, g_ffn, w_router, b_router, w_gate, b_gate, w_up, b_up, w_down, b_down) = lw
    bp, lp, bs, ls, past = geom
    tp = bp * lp
    d = x_src[0].shape[1]
    q_lora, n_heads, qk_dim = w_uq.shape
    kv_lora, _, nope = w_uk.shape
    v_dim = w_uv.shape[2]
    rope = qk_dim - nope
    c = conv_w.shape[1]
    assert nope == LANES and v_dim == LANES and 2 * rope == LANES and tp % ls == 0

    o1, o2, o3, o4 = q_lora + kv_lora, q_lora + kv_lora + rope, q_lora + kv_lora + rope + c, q_lora + kv_lora + rope + 2 * c
    w_kr = w_in[:, o1:o2]
    w_head = jnp.concatenate([w_in[:, :o1], w_kr, _swap_halves(w_kr)], axis=1).astype(BF16)
    w_xb = w_in[:, o2:o3].astype(BF16)
    w_yb = w_in[:, o3:o4].astype(BF16)
    w_mg = w_in[:, o4:].astype(BF16)
    w_qr = w_uq[:, :, nope:]
    w_q = jnp.concatenate([w_uq[:, :, :nope], w_qr, _swap_halves(w_qr)], axis=2).reshape(q_lora, -1).astype(BF16)
    w_ukv = jnp.concatenate([w_uk.reshape(kv_lora, -1), w_uv.reshape(kv_lora, -1)], axis=1).astype(BF16)

    xn = _rmsnorm_bf16(x_src, g_mix)
    cqn, ckv, ckv_b, kr, kr_b = _head_proj(xn, w_head, g_q, g_kv, cc, ss)
    xb = _matmul([(xn, w_xb)], _ep_identity, F32, name="proj_xb")
    gy = _matmul([(xn, w_yb)], _ep_gelu, BF16, name="proj_yb")
    gates = _matmul([(xn, w_mg)], _ep_sigmoid, BF16, name="proj_gates")

    scale = float(qk_dim) ** -0.5 * math.log2(math.e)
    q = _matmul([(cqn, w_q)], functools.partial(_ep_q_rope, scale=scale), BF16, row_extras=(cc, ss), name="q_up")
    kv_p = _matmul([(ckv_b[:tp], w_ukv)], _ep_identity, BF16, name="kv_up_prompt")
    o_p = _attention(q, 0, kv_p, kr_b[:tp], bp, n_heads, lp, lp, 0)
    lk_s = past + ls
    ckv_all = jnp.concatenate([past_ckv.astype(BF16), ckv_b[tp:].reshape(bs, ls, kv_lora)], axis=1)
    kr_past = jnp.pad(past_krope, ((0, 0), (0, 0), (0, LANES - rope))).astype(BF16)
    kr_all = jnp.concatenate([kr_past, kr_b[tp:].reshape(bs, ls, LANES)], axis=1)
    o_s = _attention_absorbed(q, tp // ls, ckv_all.reshape(bs * lk_s, kv_lora), kr_all.reshape(bs * lk_s, LANES),
                              w_ukv[:, :n_heads * nope], w_ukv[:, n_heads * nope:], bs, n_heads, ls, lk_s, past)

    hg_p, lru_p = _rglru(xb, gy, 0, bp, lp, jnp.zeros((bp, CONV_W - 1, c), F32), jnp.zeros((bp, c), F32),
                         conv_w, conv_b, w_ra, b_ra, w_ri, b_ri, lam, True)
    hg_s, lru_s_new = _rglru(xb, gy, tp, bs, ls, conv_s, lru_s, conv_w, conv_b, w_ra, b_ra, w_ri, b_ri, lam, past == 0)
    conv_p_new = _conv_tail(jnp.zeros((bp, CONV_W - 1, c), F32), xb, 0, bp, lp)
    conv_s_new = _conv_tail(conv_s.astype(F32), xb, tp, bs, ls)

    mix = _matmul([((o_p, o_s), w_o_attn.astype(BF16)), ((hg_p, hg_s), w_o_rnn.astype(BF16))], _ep_merge, BF16,
                  tile_extras=((gates, 0), (gates, d)), tn=512, name="merge")
    h1, h1_rows = _matmul([(mix, w_out.astype(BF16))], _ep_residual, F32, tile_extras=((x_src, 0),),
                          row_major_copy=True, name="out_proj")

    h2 = _moe(h1, h1_rows, tp, g_ffn, w_router, b_router, w_gate, b_gate, w_up, b_up, w_down, b_down, g_out,
              final_norm)
    states_p = (ckv[:tp].reshape(bp, lp, kv_lora), kr[:tp, :rope].reshape(bp, lp, rope), conv_p_new, lru_p)
    states_s = (ckv[tp:].reshape(bs, ls, kv_lora), kr[tp:, :rope].reshape(bs, ls, rope), conv_s_new, lru_s_new)
    return h2, states_p, states_s


def kernel(x_prompt, x_sample, cache_ckv, cache_krope, state_conv, state_lru, g_mix, w_in, g_q, w_uq, g_kv, w_uk, w_uv, w_o_attn, conv_w, conv_b, w_ra, b_ra, w_ri, b_ri, lam, w_o_rnn, w_out, g_ffn, w_router, b_router, w_gate, b_gate, w_up, b_up, w_down, b_down, g_final):
    bp, lp, d = x_prompt.shape
    bs, ls, _ = x_sample.shape
    depth, _, past, _ = cache_ckv.shape
    rope = cache_krope.shape[-1]
    geom = (bp, lp, bs, ls, past)
    pos = jnp.concatenate([jnp.tile(jnp.arange(lp, dtype=jnp.int32), bp),
                           jnp.tile(past + jnp.arange(ls, dtype=jnp.int32), bs)])
    cc, ss = _rope_tables(pos, rope)
    h = (x_prompt.reshape(bp * lp, d), x_sample.reshape(bs * ls, d))
    weights = (g_mix, w_in, g_q, w_uq, g_kv, w_uk, w_uv, w_o_attn, conv_w, conv_b, w_ra, b_ra, w_ri, b_ri,
               lam, w_o_rnn, w_out, g_ffn, w_router, b_router, w_gate, b_gate, w_up, b_up, w_down, b_down)
    st_p, st_s = [], []
    for layer in range(depth):
        lw = tuple(w[layer] for w in weights)
        last = layer == depth - 1
        h, sp, ss_ = _layer(h, geom, cc, ss, cache_ckv[layer], cache_krope[layer], state_conv[layer],
                            state_lru[layer], lw, g_final if last else g_mix[layer], last)
        st_p.append(sp)
        st_s.append(ss_)
    outs = [h[0].reshape(bp, lp, d), h[1].reshape(bs, ls, d)]
    for group in (st_p, st_s):
        for k in range(4):
            outs.append(jnp.stack([s[k] for s in group]))
    return tuple(outs)
```
